```python
import math
import jax, jax.numpy as jnp
from jax import lax
import numpy as np

D_MODEL = 1024
BATCH = 16
SEQ = 4096
DEPTH = 2

HEAD_DIM = 64
Q_BLOCK = 128
A_HEADS = 4
IDX_HEADS = 4
IDX_DIM = 32
DSA_TOPK = 256
B_HEADS = 4
N_NSA_KV = 6
CMP_BLOCK = 32
CMP_STRIDE = 16
CMP_HIDDEN = 256
SLC_BLOCK = 64
SLC_TOPN = 16
WINDOW = 512
FORCE_SCORE = 1e9
C_HEADS = 4
C_VDIM = 2 * HEAD_DIM
N_BUCKETS = 32
MAX_DISTANCE = 128
N_BIAS_HEADS = A_HEADS + B_HEADS + C_HEADS
N_EXPERTS = 32
TOP_K = 4
D_FF = D_MODEL
SWIGLU_LIMIT = 7.0
SWIGLU_ALPHA = 1.702
DN_ALPHA = (2 * DEPTH) ** 0.25
DN_BETA = (8 * DEPTH) ** -0.25
LN_EPS = 1e-5
A_W = A_HEADS * HEAD_DIM
B_W = B_HEADS * HEAD_DIM
C_W = C_HEADS * C_VDIM
IN_SIZES = (A_HEADS * HEAD_DIM, HEAD_DIM, HEAD_DIM,
            IDX_HEADS * IDX_DIM, IDX_DIM, IDX_HEADS,
            B_HEADS * HEAD_DIM, N_NSA_KV * HEAD_DIM, 3 * B_HEADS,
            C_HEADS * 2 * HEAD_DIM, C_HEADS * 2 * HEAD_DIM, C_W,
            3 * D_MODEL)
D_IN = sum(IN_SIZES)

kernel_name = "hybrid_dsa_nsa_diff_moe_deepnorm_adaln"


def _split_points():
    return [int(v) for v in np.cumsum(IN_SIZES)[:-1]]


def _normal(key, shape, scale):
    return jax.random.normal(key, shape, jnp.float32) * scale


def layer_norm(x, g, b):
    xf = x.astype(jnp.float32)
    mu = jnp.mean(xf, axis=-1, keepdims=True)
    var = jnp.mean(jnp.square(xf - mu), axis=-1, keepdims=True)
    return ((xf - mu) * lax.rsqrt(var + LN_EPS) * g + b).astype(x.dtype)


def rms_norm(x, g):
    xf = x.astype(jnp.float32)
    return xf * lax.rsqrt(jnp.mean(jnp.square(xf), axis=-1, keepdims=True) + LN_EPS) * g


def masked_softmax(logits, mask):
    lg = jnp.where(mask, logits.astype(jnp.float32), -jnp.inf)
    m = jnp.max(lg, axis=-1, keepdims=True)
    m = jnp.where(jnp.isfinite(m), m, 0.0)
    p = jnp.exp(lg - m)
    return p / jnp.maximum(jnp.sum(p, axis=-1, keepdims=True), 1e-30)


def rel_bucket(dist):
    exact = N_BUCKETS // 2
    n = jnp.maximum(dist, 0)
    nf = jnp.maximum(n, 1).astype(jnp.float32)
    large = exact + (jnp.log(nf / exact) / math.log(MAX_DISTANCE / exact)
                     * (N_BUCKETS - exact)).astype(jnp.int32)
    large = jnp.minimum(large, N_BUCKETS - 1)
    return jnp.where(n < exact, n, large)


def _gather_rows(table, idx):
    return jax.vmap(lambda tb, ib: tb[ib])(table, idx)


def _slc_overlap(n_cmp, n_slc):
    start = np.arange(n_cmp) * CMP_STRIDE
    end = start + CMP_BLOCK
    bs = np.arange(n_slc) * SLC_BLOCK
    ov = (start[:, None] < bs[None, :] + SLC_BLOCK) & (end[:, None] > bs[None, :])
    return ov.astype(np.float32)


def adaln(c, w, b):
    mod = jax.nn.silu(c) @ w + b
    shift, scale, gate = jnp.split(mod, 3, axis=-1)
    return shift[:, None, :], scale[:, None, :], gate[:, None, :]


def hybrid_mixer(h, w_in, cmp_pos, cmp_w1, cmp_w2, diff_lambda, diff_norm_g,
                 w_branch_a, w_branch_b, w_branch_c, w_out, rel_bias, lam_init):
    Bn, S, _ = h.shape
    f32 = jnp.float32
    scale = HEAD_DIM ** -0.5
    proj = h @ w_in
    (aq, ak, av, iq, ik, iw, bq, bkv, bg, cq, ck, cv, mg) = jnp.split(proj, _split_points(), axis=-1)
    aq = aq.reshape(Bn, S, A_HEADS, HEAD_DIM)
    iq = iq.reshape(Bn, S, IDX_HEADS, IDX_DIM)
    bq = bq.reshape(Bn, S, B_HEADS, HEAD_DIM)
    bkv = bkv.reshape(Bn, S, N_NSA_KV, HEAD_DIM)
    k_cmp_raw, v_cmp_raw, k_slc, v_slc, k_win, v_win = [bkv[:, :, i] for i in range(N_NSA_KV)]
    bg = jax.nn.sigmoid(bg.reshape(Bn, S, B_HEADS, 3).astype(f32))
    cq = cq.reshape(Bn, S, C_HEADS, 2, HEAD_DIM)
    ck = ck.reshape(Bn, S, C_HEADS, 2, HEAD_DIM)
    cv = cv.reshape(Bn, S, C_HEADS, C_VDIM)

    n_cmp = (S - CMP_BLOCK) // CMP_STRIDE + 1
    cmp_idx = np.arange(n_cmp)[:, None] * CMP_STRIDE + np.arange(CMP_BLOCK)[None, :]
    cmp_last = jnp.asarray(cmp_idx[:, -1], jnp.int32)

    def compress(raw, j):
        blocks = raw[:, cmp_idx] + cmp_pos[j]
        hid = jax.nn.gelu(blocks.reshape(Bn, n_cmp, CMP_BLOCK * HEAD_DIM) @ cmp_w1[j])
        return hid @ cmp_w2[j]

    k_cmp = compress(k_cmp_raw, 0)
    v_cmp = compress(v_cmp_raw, 1)
    n_slc = S // SLC_BLOCK
    overlap = jnp.asarray(_slc_overlap(n_cmp, n_slc))
    k_slc_blk = k_slc.reshape(Bn, n_slc, SLC_BLOCK, HEAD_DIM)
    v_slc_blk = v_slc.reshape(Bn, n_slc, SLC_BLOCK, HEAD_DIM)
    k_win_pad = jnp.pad(k_win, ((0, 0), (WINDOW, 0), (0, 0)))
    v_win_pad = jnp.pad(v_win, ((0, 0), (WINDOW, 0), (0, 0)))

    k_dsa = min(DSA_TOPK, S // 4)
    n_sel = min(SLC_TOPN, n_slc)
    tab_a = rel_bias[:, :A_HEADS]
    tab_b = rel_bias[:, A_HEADS:A_HEADS + B_HEADS]
    tab_c = rel_bias[:, A_HEADS + B_HEADS:]
    dl = diff_lambda.astype(f32)
    lam = jnp.exp(jnp.sum(dl[0] * dl[1])) - jnp.exp(jnp.sum(dl[2] * dl[3])) + lam_init
    keys = jnp.arange(S, dtype=jnp.int32)

    def block_fn(qi):
        q0 = qi * Q_BLOCK
        t = q0 + jnp.arange(Q_BLOCK, dtype=jnp.int32)
        sl = lambda a: lax.dynamic_slice_in_dim(a, q0, Q_BLOCK, axis=1)
        causal = keys[None, :] <= t[:, None]

        isc = jnp.einsum('bthd,bsd->bths', sl(iq), ik) * (IDX_DIM ** -0.5)
        iscore = jnp.einsum('bth,bths->bts', sl(iw) * (IDX_HEADS ** -0.5), jax.nn.relu(isc))
        iscore = jnp.where(causal[None], iscore.astype(f32), -jnp.inf)
        top_val, top_idx = lax.top_k(iscore, k_dsa)
        valid_a = jnp.isfinite(top_val)
        ka = _gather_rows(ak, top_idx)
        va = _gather_rows(av, top_idx)
        la = jnp.einsum('bthd,btkd->bhtk', sl(aq), ka) * scale
        la = la + jnp.transpose(tab_a[rel_bucket(t[None, :, None] - top_idx)], (0, 3, 1, 2))
        pa = masked_softmax(la, valid_a[:, None])
        oa = jnp.einsum('bhtk,btkd->bthd', pa, va).reshape(Bn, Q_BLOCK, A_W)

        qb = sl(bq)
        lc = jnp.einsum('bthd,bnd->bhtn', qb, k_cmp) * scale
        cmask = cmp_last[None, :] <= t[:, None]
        p_cmp = masked_softmax(lc, cmask[None, None])
        o_cmp = jnp.einsum('bhtn,bnd->bthd', p_cmp, v_cmp)
        imp = jnp.einsum('bhtn,nj->btj', p_cmp, overlap)
        blk = jnp.arange(n_slc, dtype=jnp.int32)
        cur = t // SLC_BLOCK
        forced = (blk[None] == 0) | (blk[None] == cur[:, None]) | (blk[None] == cur[:, None] - 1)
        allowed = blk[None] <= cur[:, None]
        imp = jnp.where(allowed[None], jnp.where(forced[None], FORCE_SCORE, imp), -jnp.inf)
        sv, sidx = lax.top_k(imp, n_sel)
        ks = _gather_rows(k_slc_blk, sidx).reshape(Bn, Q_BLOCK, n_sel * SLC_BLOCK, HEAD_DIM)
        vs = _gather_rows(v_slc_blk, sidx).reshape(Bn, Q_BLOCK, n_sel * SLC_BLOCK, HEAD_DIM)
        tok = (sidx[..., None] * SLC_BLOCK + jnp.arange(SLC_BLOCK, dtype=jnp.int32)).reshape(Bn, Q_BLOCK, -1)
        smask = jnp.repeat(jnp.isfinite(sv), SLC_BLOCK, axis=-1) & (tok <= t[None, :, None])
        ls = jnp.einsum('bthd,btmd->bhtm', qb, ks) * scale
        ls = ls + jnp.transpose(tab_b[rel_bucket(t[None, :, None] - tok)], (0, 3, 1, 2))
        o_slc = jnp.einsum('bhtm,btmd->bthd', masked_softmax(ls, smask[:, None]), vs)
        kw = lax.dynamic_slice_in_dim(k_win_pad, q0, Q_BLOCK + WINDOW, axis=1)
        vw = lax.dynamic_slice_in_dim(v_win_pad, q0, Q_BLOCK + WINDOW, axis=1)
        pos = q0 - WINDOW + jnp.arange(Q_BLOCK + WINDOW, dtype=jnp.int32)
        dw = t[:, None] - pos[None, :]
        wmask = (dw >= 0) & (dw < WINDOW) & (pos[None, :] >= 0)
        lw = jnp.einsum('bthd,bsd->bhts', qb, kw) * scale
        lw = lw + jnp.transpose(tab_b[rel_bucket(dw)], (2, 0, 1))[None]
        o_win = jnp.einsum('bhts,bsd->bthd', masked_softmax(lw, wmask[None, None]), vw)
        g = sl(bg)
        ob = (g[..., 0:1] * o_cmp + g[..., 1:2] * o_slc + g[..., 2:3] * o_win).reshape(Bn, Q_BLOCK, B_W)

        lcd = jnp.einsum('bthjd,bshjd->bjhts', sl(cq), ck) * scale
        lcd = lcd + jnp.transpose(tab_c[rel_bucket(t[:, None] - keys[None, :])], (2, 0, 1))[None, None]
        pc = masked_softmax(lcd, causal[None, None, None])
        attn = pc[:, 0] - lam * pc[:, 1]
        oc = jnp.einsum('bhts,bshd->bthd', attn, cv)
        oc = (rms_norm(oc, diff_norm_g) * (1.0 - lam_init)).reshape(Bn, Q_BLOCK, C_W)
        return oa, ob, oc

    oa, ob, oc = lax.map(block_fn, jnp.arange(S // Q_BLOCK))
    unblock = lambda o: jnp.swapaxes(o, 0, 1).reshape(Bn, S, -1).astype(h.dtype)
    ya = unblock(oa) @ w_branch_a
    yb = unblock(ob) @ w_branch_b
    yc = unblock(oc) @ w_branch_c
    gates = jax.nn.sigmoid(mg.reshape(Bn, S, 3, D_MODEL))
    merged = gates[:, :, 0] * ya + gates[:, :, 1] * yb + gates[:, :, 2] * yc
    return merged @ w_out


def moe_ffn(h, router_w, router_b, w_gu, b_gu, w_down, b_down):
    Bn, S, D = h.shape
    tok = h.reshape(Bn * S, D)
    logits = (tok @ router_w + router_b).astype(jnp.float32)
    top_v, top_i = lax.top_k(logits, TOP_K)
    top_w = jax.nn.softmax(top_v, axis=-1)
    combine = jnp.sum(jax.nn.one_hot(top_i, N_EXPERTS, dtype=jnp.float32) * top_w[..., None], axis=1)
    out = jnp.zeros((Bn * S, D), jnp.float32)
    for e in range(N_EXPERTS):
        gu = tok @ w_gu[e] + b_gu[e]
        gate = jnp.minimum(gu[:, 0::2], SWIGLU_LIMIT)
        up = jnp.clip(gu[:, 1::2], -SWIGLU_LIMIT, SWIGLU_LIMIT)
        act = (up + 1.0) * (gate * jax.nn.sigmoid(SWIGLU_ALPHA * gate))
        out = out + combine[:, e:e + 1] * (act @ w_down[e] + b_down[e])
    return out.reshape(Bn, S, D).astype(h.dtype)


def setup_inputs(seed: int = 0) -> dict:
    key = jax.random.key(seed)
    ks = jax.random.split(key, 32)
    L, D = DEPTH, D_MODEL
    cmp_in = CMP_BLOCK * HEAD_DIM
    return {
        "x": _normal(ks[0], (BATCH, SEQ, D), 1.0),
        "c": _normal(ks[1], (BATCH, D), 1.0),
        "rel_bias": _normal(ks[2], (N_BUCKETS, N_BIAS_HEADS), 0.5),
        "mod_attn_w": _normal(ks[3], (L, D, 3 * D), 0.1 * D ** -0.5),
        "mod_attn_b": _normal(ks[4], (L, 3 * D), 0.02),
        "w_in": _normal(ks[5], (L, D, D_IN), D ** -0.5),
        "cmp_pos": _normal(ks[6], (L, 2, CMP_BLOCK, HEAD_DIM), 0.1),
        "cmp_w1": _normal(ks[7], (L, 2, cmp_in, CMP_HIDDEN), cmp_in ** -0.5),
        "cmp_w2": _normal(ks[8], (L, 2, CMP_HIDDEN, HEAD_DIM), CMP_HIDDEN ** -0.5),
        "diff_lambda": _normal(ks[9], (L, 4, HEAD_DIM), 0.1),
        "diff_norm_g": 1.0 + _normal(ks[10], (L, C_VDIM), 0.02),
        "w_branch_a": _normal(ks[11], (L, A_W, D), A_W ** -0.5),
        "w_branch_b": _normal(ks[12], (L, B_W, D), B_W ** -0.5),
        "w_branch_c": _normal(ks[13], (L, C_W, D), C_W ** -0.5),
        "w_out": _normal(ks[14], (L, D, D), DN_BETA * D ** -0.5),
        "ln1_g": 1.0 + _normal(ks[15], (L, D), 0.02),
        "ln1_b": _normal(ks[16], (L, D), 0.02),
        "mod_ffn_w": _normal(ks[17], (L, D, 3 * D), 0.1 * D ** -0.5),
        "mod_ffn_b": _normal(ks[18], (L, 3 * D), 0.02),
        "router_w": _normal(ks[19], (L, D, N_EXPERTS), D ** -0.5),
        "router_b": _normal(ks[20], (L, N_EXPERTS), 0.01),
        "exp_w_gu": _normal(ks[21], (L, N_EXPERTS, D, 2 * D_FF), D ** -0.5),
        "exp_b_gu": _normal(ks[22], (L, N_EXPERTS, 2 * D_FF), 0.02),
        "exp_w_down": _normal(ks[23], (L, N_EXPERTS, D_FF, D), DN_BETA * D_FF ** -0.5),
        "exp_b_down": _normal(ks[24], (L, N_EXPERTS, D), 0.02),
        "ln2_g": 1.0 + _normal(ks[25], (L, D), 0.02),
        "ln2_b": _normal(ks[26], (L, D), 0.02),
    }


def reference(x, c, rel_bias, mod_attn_w, mod_attn_b, w_in, cmp_pos, cmp_w1, cmp_w2,
              diff_lambda, diff_norm_g, w_branch_a, w_branch_b, w_branch_c, w_out,
              ln1_g, ln1_b, mod_ffn_w, mod_ffn_b, router_w, router_b,
              exp_w_gu, exp_b_gu, exp_w_down, exp_b_down, ln2_g, ln2_b):
    for l in range(DEPTH):
        lam_init = 0.8 - 0.6 * math.exp(-0.3 * l)
        shift, scale, gate = adaln(c, mod_attn_w[l], mod_attn_b[l])
        h = x * (1.0 + scale) + shift
        y = hybrid_mixer(h, w_in[l], cmp_pos[l], cmp_w1[l], cmp_w2[l], diff_lambda[l], diff_norm_g[l],
                         w_branch_a[l], w_branch_b[l], w_branch_c[l], w_out[l], rel_bias, lam_init)
        x = layer_norm(DN_ALPHA * x + (1.0 + gate) * y, ln1_g[l], ln1_b[l])
        shift, scale, gate = adaln(c, mod_ffn_w[l], mod_ffn_b[l])
        h = x * (1.0 + scale) + shift
        y = moe_ffn(h, router_w[l], router_b[l], exp_w_gu[l], exp_b_gu[l], exp_w_down[l], exp_b_down[l])
        x = layer_norm(DN_ALPHA * x + (1.0 + gate) * y, ln2_g[l], ln2_b[l])
    return x
```

```python
import functools
import math

import numpy as np
import jax
import jax.numpy as jnp
from jax import lax
from jax.experimental import pallas as pl
from jax.experimental.pallas import tpu as pltpu

F32 = jnp.float32
_MXU_DTYPE = jnp.bfloat16

HEAD_DIM = 64
A_HEADS = 4
IDX_HEADS = 4
IDX_DIM = 32
DSA_TOPK = 256
B_HEADS = 4
CMP_BLOCK = 32
CMP_STRIDE = 16
CMP_HIDDEN = 256
SLC_BLOCK = 64
SLC_TOPN = 16
WINDOW = 512
FORCE_SCORE = 1e9
C_HEADS = 4
C_VDIM = 2 * HEAD_DIM
N_BUCKETS = 32
MAX_DISTANCE = 128
N_EXPERTS = 32
TOP_K = 4
SWIGLU_LIMIT = 7.0
SWIGLU_ALPHA = 1.702
LN_EPS = 1e-5

NEG = -1e30
INT_MIN = -2 ** 31
NEG_INF_KEY = INT_MIN + 0x7FFFFF

LANES = 128
VMEM_LIMIT_BYTES = 56 * 1024 * 1024

T_ATT = 256
TM_PROJ = 256
TM_MOE = 512


def _cparams(sem):
    return pltpu.CompilerParams(dimension_semantics=sem, vmem_limit_bytes=VMEM_LIMIT_BYTES)


def _dot(a, b):
    return jnp.dot(a, b, preferred_element_type=F32)


def _dot_nt(a, b):
    return lax.dot_general(a, b, (((1,), (1,)), ((), ())), preferred_element_type=F32)


def _adaln_kernel(c_ref, w_ref, b_ref, o_ref):
    c = c_ref[...]
    a = (c * jax.nn.sigmoid(c)).astype(_MXU_DTYPE)
    o_ref[0] = _dot(a, w_ref[0].astype(_MXU_DTYPE)) + b_ref[0]


def _adaln(c, w, b):
    G, D, N = w.shape
    Bn = c.shape[0]
    tn = 768
    return pl.pallas_call(
        _adaln_kernel,
        grid=(G, N // tn),
        in_specs=[
            pl.BlockSpec((Bn, D), lambda g, j: (0, 0)),
            pl.BlockSpec((1, D, tn), lambda g, j: (g, 0, j)),
            pl.BlockSpec((1, 1, tn), lambda g, j: (g, 0, j)),
        ],
        out_specs=pl.BlockSpec((1, Bn, tn), lambda g, j: (g, 0, j)),
        out_shape=jax.ShapeDtypeStruct((G, Bn, N), F32),
        compiler_params=_cparams(("parallel", "parallel")),
        name="adaln",
    )(c, w, b.reshape(G, 1, N))


def _inproj_kernel(x_ref, mod_ref, wa_ref, wb_ref, wc_ref, ws_ref, wg_ref,
                   oa_ref, ob_ref, oc_ref, os_ref, og_ref):
    x = x_ref[0]
    shift = mod_ref[0, 0:1, :]
    scale = mod_ref[0, 1:2, :]
    h = (x * (1.0 + scale) + shift).astype(_MXU_DTYPE)
    for w_ref, o_ref in ((wa_ref, oa_ref), (wb_ref, ob_ref), (wc_ref, oc_ref),
                         (ws_ref, os_ref), (wg_ref, og_ref)):
        n = w_ref.shape[1]
        for j in range(0, n, 512):
            w = min(512, n - j)
            o_ref[0, :, j:j + w] = _dot(h, w_ref[:, j:j + w]).astype(o_ref.dtype)


def _inproj(x, mod, wa, wb, wc, ws, wg):
    Bn, S, D = x.shape
    tm = TM_PROJ
    ws_ = [wa, wb, wc, ws, wg]
    dts = [_MXU_DTYPE, _MXU_DTYPE, _MXU_DTYPE, F32, _MXU_DTYPE]
    return pl.pallas_call(
        _inproj_kernel,
        grid=(Bn, S // tm),
        in_specs=[pl.BlockSpec((1, tm, D), lambda b, i: (b, i, 0)),
                  pl.BlockSpec((1, 3, D), lambda b, i: (b, 0, 0))]
                 + [pl.BlockSpec(w.shape, lambda b, i: (0, 0)) for w in ws_],
        out_specs=[pl.BlockSpec((1, tm, w.shape[1]), lambda b, i: (b, i, 0)) for w in ws_],
        out_shape=[jax.ShapeDtypeStruct((Bn, S, w.shape[1]), dt) for w, dt in zip(ws_, dts)],
        compiler_params=_cparams(("parallel", "parallel")),
        name="inproj",
    )(x, mod, *ws_)


def _compress_kernel(r_ref, rs_ref, pa_ref, pb_ref, wa_ref, wb_ref, w2_ref, o_ref):
    r = r_ref[0].astype(F32)
    rs = rs_ref[0].astype(F32)
    outs = []
    for j in range(2):
        xa = (r + pa_ref[j]).astype(_MXU_DTYPE)
        xb = (rs + pb_ref[j]).astype(_MXU_DTYPE)
        hid = jax.nn.gelu(_dot(xa, wa_ref[j]) + _dot(xb, wb_ref[j]))
        outs.append(_dot(hid.astype(_MXU_DTYPE), w2_ref[j]))
    o_ref[0] = jnp.concatenate(outs, axis=1).astype(o_ref.dtype)


def _compress(r2, r2s, pa, pb, wa, wb, w2):
    Bn, NC, KW = r2.shape
    return pl.pallas_call(
        _compress_kernel,
        grid=(Bn,),
        in_specs=[pl.BlockSpec((1, NC, KW), lambda b: (b, 0, 0)),
                  pl.BlockSpec((1, NC, KW), lambda b: (b, 0, 0)),
                  pl.BlockSpec(pa.shape, lambda b: (0, 0, 0)),
                  pl.BlockSpec(pb.shape, lambda b: (0, 0, 0)),
                  pl.BlockSpec(wa.shape, lambda b: (0, 0, 0)),
                  pl.BlockSpec(wb.shape, lambda b: (0, 0, 0)),
                  pl.BlockSpec(w2.shape, lambda b: (0, 0, 0))],
        out_specs=pl.BlockSpec((1, NC, 2 * HEAD_DIM), lambda b: (b, 0, 0)),
        out_shape=jax.ShapeDtypeStruct((Bn, NC, 2 * HEAD_DIM), _MXU_DTYPE),
        compiler_params=_cparams(("parallel",)),
        name="nsa_compress",
    )(r2, r2s, pa, pb, wa, wb, w2)


def _online(s, m, l, acc, v):
    m_new = jnp.maximum(m, jnp.max(s, axis=1, keepdims=True))
    alpha = jnp.exp(m - m_new)
    p = jnp.exp(s - m_new)
    l = alpha * l + jnp.sum(p, axis=1, keepdims=True)
    acc = alpha * acc + _dot(p.astype(_MXU_DTYPE), v)
    return m_new, l, acc


def _finish(m, l, acc):
    return jnp.where(m > 0.5 * NEG, acc / jnp.maximum(l, 1e-30), 0.0)


def _stack_heads(x, n_heads, width):
    return jnp.concatenate([x[:, h * width:(h + 1) * width] for h in range(n_heads)], axis=0)


def _unstack_heads(x, n_heads):
    t = x.shape[0] // n_heads
    return jnp.concatenate([x[h * t:(h + 1) * t] for h in range(n_heads)], axis=1)


def _dsa_kernel(aq_ref, akv_ref, iq_ref, ik_ref, side_ref, bias_ref, cmask_ref, o_ref, keys_ref,
                *, T, topk):
    qi = pl.program_id(1)
    nk = qi + 1
    H = A_HEADS
    scale = HEAD_DIM ** -0.5

    q_st = _stack_heads(aq_ref[0], H, HEAD_DIM)
    iq_st = _stack_heads(iq_ref[0], IDX_HEADS, IDX_DIM)
    iw = side_ref[0][:, 0:IDX_HEADS] * (IDX_HEADS ** -0.5)

    def score_chunk(kc, carry):
        start = pl.multiple_of(kc * T, T)
        ikc = ik_ref[0, pl.ds(start, T), 0:IDX_DIM]
        sc = jnp.maximum(_dot_nt(iq_st, ikc) * (IDX_DIM ** -0.5), 0.0)
        isc = iw[:, 0:1] * sc[0:T]
        for h in range(1, IDX_HEADS):
            isc = isc + iw[:, h:h + 1] * sc[h * T:(h + 1) * T]
        isc = (isc + 0.0) + cmask_ref[jnp.minimum(qi - kc, 1)]
        bits = pltpu.bitcast(isc, jnp.int32)
        keys_ref[kc] = jnp.where(bits < 0, bits ^ 0x7FFFFFFF, bits)
        return carry

    lax.fori_loop(0, nk, score_chunk, 0)

    def count(pred):
        def body(kc, acc):
            return acc + jnp.where(pred(keys_ref[kc]), 1.0, 0.0)
        acc = lax.fori_loop(0, nk, body, jnp.zeros((T, T), F32))
        return jnp.sum(acc, axis=1, keepdims=True)

    def bit_step(b, thr):
        cand = thr + lax.shift_left(jnp.int32(1), 31 - b)
        cnt = count(lambda k: k >= cand)
        return jnp.where(cnt >= float(topk), cand, thr)

    thr = lax.fori_loop(0, 32, bit_step, jnp.full((T, 1), INT_MIN, jnp.int32))
    need = float(topk) - count(lambda k: k > thr)

    row = lax.broadcasted_iota(jnp.int32, (T, T), 0)
    col = lax.broadcasted_iota(jnp.int32, (T, T), 1)
    upper = jnp.where(row < col, 1.0, 0.0).astype(_MXU_DTYPE)

    def attend(kc, carry):
        m, l, acc, run = carry
        start = pl.multiple_of(kc * T, T)
        keys = keys_ref[kc]
        eq = keys == thr
        eqf = jnp.where(eq, 1.0, 0.0)
        prefix = _dot(eqf.astype(_MXU_DTYPE), upper) + run
        sel = ((keys > thr) | (eq & (prefix < need))) & (keys != NEG_INF_KEY)
        run = run + jnp.sum(eqf, axis=1, keepdims=True)
        kch = akv_ref[0, pl.ds(start, T), 0:HEAD_DIM]
        vch = akv_ref[0, pl.ds(start, T), HEAD_DIM:2 * HEAD_DIM]
        s = _dot_nt(q_st, kch) * scale + bias_ref[jnp.minimum(qi - kc, 2)]
        sel_st = jnp.concatenate([sel] * H, axis=0)
        s = jnp.where(sel_st, s, NEG)
        m, l, acc = _online(s, m, l, acc, vch)
        return m, l, acc, run

    init = (jnp.full((H * T, 1), NEG, F32), jnp.zeros((H * T, 1), F32),
            jnp.zeros((H * T, HEAD_DIM), F32), jnp.zeros((T, 1), F32))
    m, l, acc, _ = lax.fori_loop(0, nk, attend, init)
    o_ref[0] = _unstack_heads(_finish(m, l, acc), H).astype(o_ref.dtype)


def _dsa(apack, side, bias, cmask, T):
    Bn, S, _ = apack.shape
    topk = min(DSA_TOPK, S // 4)
    kern = functools.partial(_dsa_kernel, T=T, topk=topk)
    return pl.pallas_call(
        kern,
        grid=(Bn, S // T),
        in_specs=[pl.BlockSpec((1, T, 256), lambda b, i: (b, i, 0)),
                  pl.BlockSpec((1, S, 128), lambda b, i: (b, 0, 2)),
                  pl.BlockSpec((1, T, 128), lambda b, i: (b, i, 3)),
                  pl.BlockSpec((1, S, 128), lambda b, i: (b, 0, 4)),
                  pl.BlockSpec((1, T, 128), lambda b, i: (b, i, 0)),
                  pl.BlockSpec(bias.shape, lambda b, i: (0, 0, 0)),
                  pl.BlockSpec(cmask.shape, lambda b, i: (0, 0, 0))],
        out_specs=pl.BlockSpec((1, T, 256), lambda b, i: (b, i, 0)),
        out_shape=jax.ShapeDtypeStruct((Bn, S, 256), _MXU_DTYPE),
        scratch_shapes=[pltpu.VMEM((S // T, T, T), jnp.int32)],
        compiler_params=_cparams(("parallel", "arbitrary")),
        name="dsa_attention",
    )(apack, apack, apack, apack, side, bias, cmask)


def _nsa_kernel(bq_ref, kvc_ref, kvs_ref, kvw_ref, side_ref, bias_ref, wbias_ref, ov_ref, o_ref,
                *, T, n_cmp, n_sel, n_win):
    qi = pl.program_id(1)
    nk = qi + 1
    H = B_HEADS
    scale = HEAD_DIM ** -0.5
    NC = kvc_ref.shape[1]
    NSP = ov_ref.shape[1]

    q_st = _stack_heads(bq_ref[0], H, HEAD_DIM)
    t_row = qi * T + lax.broadcasted_iota(jnp.int32, (T, 1), 0)
    t_st = jnp.concatenate([t_row] * H, axis=0)

    kvc = kvc_ref[0]
    n_idx = lax.broadcasted_iota(jnp.int32, (1, NC), 1)
    cvalid = (n_idx * CMP_STRIDE + (CMP_BLOCK - 1) <= t_st) & (n_idx < n_cmp)
    lc = jnp.where(cvalid, _dot_nt(q_st, kvc[:, 0:HEAD_DIM]) * scale, NEG)
    mc = jnp.max(lc, axis=1, keepdims=True)
    pc = jnp.where(cvalid, jnp.exp(lc - mc), 0.0)
    pc = pc / jnp.maximum(jnp.sum(pc, axis=1, keepdims=True), 1e-30)
    o_cmp = _dot(pc.astype(_MXU_DTYPE), kvc[:, HEAD_DIM:2 * HEAD_DIM])

    psum = pc[0:T]
    for h in range(1, H):
        psum = psum + pc[h * T:(h + 1) * T]
    p_hi = psum.astype(_MXU_DTYPE)
    p_lo = (psum - p_hi.astype(F32)).astype(_MXU_DTYPE)
    imp = _dot(p_hi, ov_ref[...]) + _dot(p_lo, ov_ref[...])

    blk = lax.broadcasted_iota(jnp.int32, (T, NSP), 1)
    blkf = blk.astype(F32)
    cur = lax.shift_right_logical(t_row, int(math.log2(SLC_BLOCK)))
    forced = (blk == 0) | (blk == cur) | (blk == cur - 1)
    val = jnp.where(blk <= cur, jnp.where(forced, FORCE_SCORE, imp), -jnp.inf)

    def pick_one(_, carry):
        val, sel = carry
        mx = jnp.max(val, axis=1, keepdims=True)
        first = jnp.min(jnp.where(val == mx, blkf, float(NSP)), axis=1, keepdims=True)
        pick = blkf == first
        sel = jnp.where(pick & (mx > -jnp.inf), 1.0, sel)
        return jnp.where(pick, -jnp.inf, val), sel

    _, sel = lax.fori_loop(0, n_sel, pick_one, (val, jnp.zeros((T, NSP), F32)))
    sel = sel.astype(_MXU_DTYPE)

    e_row = lax.broadcasted_iota(jnp.int32, (NSP, T), 0)
    e_col = lax.shift_right_logical(lax.broadcasted_iota(jnp.int32, (NSP, T), 1),
                                    int(math.log2(SLC_BLOCK)))
    bpc = T // SLC_BLOCK

    init = (jnp.full((H * T, 1), NEG, F32), jnp.zeros((H * T, 1), F32),
            jnp.zeros((H * T, HEAD_DIM), F32))

    def slc_chunk(kc, carry):
        start = pl.multiple_of(kc * T, T)
        expand = jnp.where(e_row == kc * bpc + e_col, 1.0, 0.0).astype(_MXU_DTYPE)
        hit = _dot(sel, expand)
        addm = jnp.where(hit > 0.5, 0.0, NEG)
        kch = kvs_ref[0, pl.ds(start, T), 0:HEAD_DIM]
        vch = kvs_ref[0, pl.ds(start, T), HEAD_DIM:2 * HEAD_DIM]
        s = (_dot_nt(q_st, kch) * scale + bias_ref[jnp.minimum(qi - kc, 2)]
             + jnp.concatenate([addm] * H, axis=0))
        return _online(s, *carry, vch)

    o_slc = _finish(*lax.fori_loop(0, nk, slc_chunk, init))

    carry = init
    for delta in range(n_win):
        kc = qi - delta
        start = pl.multiple_of(jnp.maximum(kc, 0) * T, T)
        kch = kvw_ref[0, pl.ds(start, T), 0:HEAD_DIM]
        vch = kvw_ref[0, pl.ds(start, T), HEAD_DIM:2 * HEAD_DIM]
        s = (_dot_nt(q_st, kch) * scale + wbias_ref[delta]) + jnp.where(kc >= 0, 0.0, NEG)
        carry = _online(s, *carry, vch)
    o_win = _finish(*carry)

    g = jax.nn.sigmoid(side_ref[0][:, IDX_HEADS:IDX_HEADS + 3 * H])
    outs = []
    for h in range(H):
        sl = slice(h * T, (h + 1) * T)
        outs.append(g[:, 3 * h:3 * h + 1] * o_cmp[sl] + g[:, 3 * h + 1:3 * h + 2] * o_slc[sl]
                    + g[:, 3 * h + 2:3 * h + 3] * o_win[sl])
    o_ref[0] = jnp.concatenate(outs, axis=1).astype(o_ref.dtype)


def _nsa(bpack, kvc, side, bias, wbias, overlap, T, n_cmp, n_sel):
    Bn, S, _ = bpack.shape
    NC = kvc.shape[1]
    kern = functools.partial(_nsa_kernel, T=T, n_cmp=n_cmp, n_sel=n_sel, n_win=wbias.shape[0])
    return pl.pallas_call(
        kern,
        grid=(Bn, S // T),
        in_specs=[pl.BlockSpec((1, T, 256), lambda b, i: (b, i, 0)),
                  pl.BlockSpec((1, NC, 128), lambda b, i: (b, 0, 0)),
                  pl.BlockSpec((1, S, 128), lambda b, i: (b, 0, 3)),
                  pl.BlockSpec((1, S, 128), lambda b, i: (b, 0, 4)),
                  pl.BlockSpec((1, T, 128), lambda b, i: (b, i, 0)),
                  pl.BlockSpec(bias.shape, lambda b, i: (0, 0, 0)),
                  pl.BlockSpec(wbias.shape, lambda b, i: (0, 0, 0)),
                  pl.BlockSpec(overlap.shape, lambda b, i: (0, 0))],
        out_specs=pl.BlockSpec((1, T, 256), lambda b, i: (b, i, 0)),
        out_shape=jax.ShapeDtypeStruct((Bn, S, 256), _MXU_DTYPE),
        compiler_params=_cparams(("parallel", "arbitrary")),
        name="nsa_attention",
    )(bpack, kvc, bpack, bpack, side, bias, wbias, overlap)


def _diff_kernel(cq_ref, ck_ref, cv_ref, bias_ref, dl_ref, g_ref, o_ref, m_ref, l_ref, acc_ref,
                 *, T, lam_init):
    qi = pl.program_id(1)
    nk = qi + 1
    H = C_HEADS
    scale = HEAD_DIM ** -0.5
    dl = dl_ref[...]
    lam = (jnp.exp(jnp.sum(dl[0:1] * dl[1:2], axis=1, keepdims=True))
           - jnp.exp(jnp.sum(dl[2:3] * dl[3:4], axis=1, keepdims=True)) + lam_init)

    m_ref[...] = jnp.full(m_ref.shape, NEG, F32)
    l_ref[...] = jnp.zeros(l_ref.shape, F32)
    acc_ref[...] = jnp.zeros(acc_ref.shape, F32)
    cq = cq_ref[0]

    def chunk(kc, carry):
        start = pl.multiple_of(kc * T, T)
        dsel = jnp.minimum(qi - kc, 2)
        kch = ck_ref[0, pl.ds(start, T), :]
        vch = cv_ref[0, pl.ds(start, T), :]
        for h in range(H):
            b = bias_ref[dsel, h]
            v = vch[:, h * C_VDIM:(h + 1) * C_VDIM]
            for j in range(2):
                c0 = h * 2 * HEAD_DIM + j * HEAD_DIM
                s = _dot_nt(cq[:, c0:c0 + HEAD_DIM], kch[:, c0:c0 + HEAD_DIM]) * scale + b
                n = 2 * h + j
                m, l, acc = _online(s, m_ref[n], l_ref[n], acc_ref[n], v)
                m_ref[n] = m
                l_ref[n] = l
                acc_ref[n] = acc
        return carry

    lax.fori_loop(0, nk, chunk, 0)

    outs = []
    for h in range(H):
        o1 = _finish(m_ref[2 * h], l_ref[2 * h], acc_ref[2 * h])
        o2 = _finish(m_ref[2 * h + 1], l_ref[2 * h + 1], acc_ref[2 * h + 1])
        oc = o1 - lam * o2
        rms = lax.rsqrt(jnp.mean(jnp.square(oc), axis=1, keepdims=True) + LN_EPS)
        outs.append(oc * rms * g_ref[...] * (1.0 - lam_init))
    o_ref[0] = jnp.concatenate(outs, axis=1).astype(o_ref.dtype)


def _diff(cpack, bias, dl, g, T, lam_init):
    Bn, S, _ = cpack.shape
    W = C_HEADS * C_VDIM
    kern = functools.partial(_diff_kernel, T=T, lam_init=lam_init)
    return pl.pallas_call(
        kern,
        grid=(Bn, S // T),
        in_specs=[pl.BlockSpec((1, T, W), lambda b, i: (b, i, 0)),
                  pl.BlockSpec((1, S, W), lambda b, i: (b, 0, 1)),
                  pl.BlockSpec((1, S, W), lambda b, i: (b, 0, 2)),
                  pl.BlockSpec(bias.shape, lambda b, i: (0, 0, 0, 0)),
                  pl.BlockSpec(dl.shape, lambda b, i: (0, 0)),
                  pl.BlockSpec(g.shape, lambda b, i: (0, 0))],
        out_specs=pl.BlockSpec((1, T, W), lambda b, i: (b, i, 0)),
        out_shape=jax.ShapeDtypeStruct((Bn, S, W), _MXU_DTYPE),
        scratch_shapes=[pltpu.VMEM((2 * C_HEADS, T, 1), F32),
                        pltpu.VMEM((2 * C_HEADS, T, 1), F32),
                        pltpu.VMEM((2 * C_HEADS, T, C_VDIM), F32)],
        compiler_params=_cparams(("parallel", "arbitrary")),
        name="diff_attention",
    )(cpack, cpack, cpack, bias, dl, g)


def _layer_norm(v, g, b):
    mu = jnp.mean(v, axis=1, keepdims=True)
    d = v - mu
    var = jnp.mean(jnp.square(d), axis=1, keepdims=True)
    return d * lax.rsqrt(var + LN_EPS) * g + b


def _split_hi_lo(v):
    hi = v.astype(_MXU_DTYPE)
    lo = (v - hi.astype(F32)).astype(_MXU_DTYPE)
    return hi, lo


def _outproj_kernel(oa_ref, ob_ref, oc_ref, mg_ref, x_ref, mod1_ref, mod2_ref,
                    wa_ref, wb_ref, wc_ref, wo_ref, ln_ref, rw_ref, rb_ref,
                    x1_ref, h2_ref, ti_ref, tw_ref, *, dn_alpha, n_experts):
    D = x_ref.shape[2]
    ya = _dot(oa_ref[0], wa_ref[...])
    yb = _dot(ob_ref[0], wb_ref[...])
    yc = _dot(oc_ref[0], wc_ref[...])
    mg = mg_ref[0].astype(F32)
    merged = (jax.nn.sigmoid(mg[:, 0:D]) * ya + jax.nn.sigmoid(mg[:, D:2 * D]) * yb
              + jax.nn.sigmoid(mg[:, 2 * D:3 * D]) * yc)
    y = _dot(merged.astype(_MXU_DTYPE), wo_ref[...])
    gate1 = mod1_ref[0, 2:3, :]
    x1 = _layer_norm(dn_alpha * x_ref[0] + (1.0 + gate1) * y, ln_ref[0:1, :], ln_ref[1:2, :])
    x1_ref[0] = x1
    h2 = x1 * (1.0 + mod2_ref[0, 1:2, :]) + mod2_ref[0, 0:1, :]
    h2_ref[0] = h2.astype(h2_ref.dtype)

    h_hi, h_lo = _split_hi_lo(h2)
    w_hi, w_lo = _split_hi_lo(rw_ref[...])
    logits = _dot(h_hi, w_hi) + _dot(h_hi, w_lo) + _dot(h_lo, w_hi) + rb_ref[...]
    lane = lax.broadcasted_iota(jnp.int32, logits.shape, 1)
    lanef = lane.astype(F32)
    work = jnp.where(lane < n_experts, logits, -jnp.inf)
    tif = jnp.zeros(logits.shape, F32)
    tv = jnp.zeros(logits.shape, F32)
    v0 = None
    for k in range(TOP_K):
        mx = jnp.max(work, axis=1, keepdims=True)
        first = jnp.min(jnp.where(work == mx, lanef, float(LANES)), axis=1, keepdims=True)
        if k == 0:
            v0 = mx
        tif = jnp.where(lane == k, first, tif)
        tv = jnp.where(lane == k, jnp.exp(mx - v0), tv)
        work = jnp.where(lanef == first, -jnp.inf, work)
    ti_ref[0] = tif.astype(jnp.int32)
    tw_ref[0] = tv / jnp.sum(tv, axis=1, keepdims=True)


def _outproj(oa, ob, oc, mg, x, mod1, mod2, wa, wb, wc, wo, ln, rw, rb, dn_alpha, n_experts):
    Bn, S, D = x.shape
    tm = TM_PROJ
    row = lambda w: pl.BlockSpec((1, tm, w), lambda b, i: (b, i, 0))
    full = lambda a: pl.BlockSpec(a.shape, lambda b, i: (0,) * a.ndim)
    modspec = pl.BlockSpec((1, 3, D), lambda b, i: (b, 0, 0))
    kern = functools.partial(_outproj_kernel, dn_alpha=dn_alpha, n_experts=n_experts)
    return pl.pallas_call(
        kern,
        grid=(Bn, S // tm),
        in_specs=[row(oa.shape[2]), row(ob.shape[2]), row(oc.shape[2]), row(mg.shape[2]), row(D),
                  modspec, modspec, full(wa), full(wb), full(wc), full(wo), full(ln), full(rw),
                  full(rb)],
        out_specs=[row(D), row(D), row(LANES), row(LANES)],
        out_shape=[jax.ShapeDtypeStruct((Bn, S, D), F32),
                   jax.ShapeDtypeStruct((Bn, S, D), _MXU_DTYPE),
                   jax.ShapeDtypeStruct((Bn, S, LANES), jnp.int32),
                   jax.ShapeDtypeStruct((Bn, S, LANES), F32)],
        compiler_params=_cparams(("parallel", "parallel")),
        name="merge_outproj_ln_router",
    )(oa, ob, oc, mg, x, mod1, mod2, wa, wb, wc, wo, ln, rw, rb)


def _moe_kernel(te_ref, tv_ref, x_ref, wg_ref, wu_ref, wd_ref, bg_ref, bu_ref, bd_ref, o_ref):
    i = pl.program_id(0)

    @pl.when(tv_ref[i] > 0)
    def _():
        x = x_ref[...]
        dff = wg_ref.shape[2]
        ck = 512
        y = jnp.zeros(o_ref.shape, F32)
        for j in range(0, dff, ck):
            gate = jnp.minimum(_dot(x, wg_ref[0, :, j:j + ck]) + bg_ref[0, :, j:j + ck], SWIGLU_LIMIT)
            up = jnp.clip(_dot(x, wu_ref[0, :, j:j + ck]) + bu_ref[0, :, j:j + ck],
                          -SWIGLU_LIMIT, SWIGLU_LIMIT)
            act = (up + 1.0) * (gate * jax.nn.sigmoid(SWIGLU_ALPHA * gate))
            y = y + _dot(act.astype(_MXU_DTYPE), wd_ref[0, j:j + ck, :])
        o_ref[...] = (y + bd_ref[0]).astype(o_ref.dtype)


def _moe(xs, tile_expert, tile_valid, wg, wu, wd, bg, bu, bd):
    R, D = xs.shape
    E, _, dff = wg.shape
    tm = TM_MOE
    nt = R // tm
    wspec = lambda a: pl.BlockSpec((1,) + a.shape[1:], lambda i, te, tv: (te[i], 0, 0))
    grid_spec = pltpu.PrefetchScalarGridSpec(
        num_scalar_prefetch=2,
        grid=(nt,),
        in_specs=[pl.BlockSpec((tm, D), lambda i, te, tv: (i, 0)),
                  wspec(wg), wspec(wu), wspec(wd), wspec(bg), wspec(bu), wspec(bd)],
        out_specs=pl.BlockSpec((tm, D), lambda i, te, tv: (i, 0)),
    )
    return pl.pallas_call(
        _moe_kernel,
        grid_spec=grid_spec,
        out_shape=jax.ShapeDtypeStruct((R, D), _MXU_DTYPE),
        compiler_params=_cparams(("arbitrary",)),
        name="moe_grouped_mlp",
    )(tile_expert, tile_valid, xs, wg, wu, wd, bg, bu, bd)


def _combine_kernel(yg_ref, tw_ref, x_ref, mod_ref, ln_ref, o_ref, *, dn_alpha):
    D = x_ref.shape[2]
    tw = tw_ref[0]
    y = tw[:, 0:1] * yg_ref[0, :, 0:D].astype(F32)
    for k in range(1, TOP_K):
        y = y + tw[:, k:k + 1] * yg_ref[0, :, k * D:(k + 1) * D].astype(F32)
    gate = mod_ref[0, 2:3, :]
    o_ref[0] = _layer_norm(dn_alpha * x_ref[0] + (1.0 + gate) * y, ln_ref[0:1, :], ln_ref[1:2, :])


def _combine(yg, tw, x, mod, ln, dn_alpha):
    Bn, S, D = x.shape
    tm = TM_PROJ
    row = lambda w: pl.BlockSpec((1, tm, w), lambda b, i: (b, i, 0))
    return pl.pallas_call(
        functools.partial(_combine_kernel, dn_alpha=dn_alpha),
        grid=(Bn, S // tm),
        in_specs=[row(TOP_K * D), row(LANES), row(D),
                  pl.BlockSpec((1, 3, D), lambda b, i: (b, 0, 0)),
                  pl.BlockSpec(ln.shape, lambda b, i: (0, 0))],
        out_specs=row(D),
        out_shape=jax.ShapeDtypeStruct((Bn, S, D), F32),
        compiler_params=_cparams(("parallel", "parallel")),
        name="moe_combine_ln",
    )(yg, tw, x, mod, ln)


def _bucket_np(dist):
    exact = N_BUCKETS // 2
    n = np.maximum(dist, 0)
    nf = np.maximum(n, 1).astype(np.float32)
    large = exact + (np.log(nf / exact) / math.log(MAX_DISTANCE / exact)
                     * (N_BUCKETS - exact)).astype(np.int32)
    large = np.minimum(large, N_BUCKETS - 1)
    return np.where(n < exact, n, large)


def _bias_tiles(tab, T, n_delta, window=None):
    i = np.arange(T)[:, None]
    j = np.arange(T)[None, :]
    tiles = []
    for delta in range(n_delta):
        d = i - j + delta * T
        ok = d >= 0 if window is None else (d >= 0) & (d < window)
        vals = jnp.transpose(tab[_bucket_np(d)], (2, 0, 1))
        tiles.append(jnp.where(jnp.asarray(ok)[None], vals, NEG))
    return jnp.stack(tiles, axis=0).astype(F32)


def _slc_overlap_np(n_cmp, n_slc, rows, cols):
    start = np.arange(n_cmp) * CMP_STRIDE
    end = start + CMP_BLOCK
    bs = np.arange(n_slc) * SLC_BLOCK
    ov = (start[:, None] < bs[None, :] + SLC_BLOCK) & (end[:, None] > bs[None, :])
    out = np.zeros((rows, cols), np.float32)
    out[:n_cmp, :n_slc] = ov
    return out


def _routing(top_i, n_experts, tm):
    N = top_i.shape[0]
    A = N * TOP_K
    nt = A // tm + n_experts
    flat_e = top_i.reshape(A)
    order = jnp.argsort(flat_e, stable=True).astype(jnp.int32)
    counts = jnp.zeros((n_experts,), jnp.int32).at[flat_e].add(1)
    padded = ((counts + tm - 1) // tm) * tm
    starts = jnp.cumsum(counts) - counts
    pstarts = jnp.cumsum(padded) - padded
    pend = pstarts + padded
    total = pend[-1]
    tile_start = jnp.arange(nt, dtype=jnp.int32) * tm
    tile_valid = (tile_start < total).astype(jnp.int32)
    tile_expert = jnp.minimum(jnp.searchsorted(pend, tile_start, side="right"),
                              n_experts - 1).astype(jnp.int32)
    last_valid = jnp.max(jnp.where(tile_valid > 0, tile_expert, 0))
    tile_expert = jnp.where(tile_valid > 0, tile_expert, last_valid)
    row = jnp.arange(nt * tm, dtype=jnp.int32)
    row_e = jnp.repeat(tile_expert, tm)
    rank = row - pstarts[row_e]
    ok = (rank < counts[row_e]) & (jnp.repeat(tile_valid, tm) > 0)
    src_slot = jnp.clip(starts[row_e] + rank, 0, A - 1)
    row_tok = jnp.where(ok, order[src_slot] // TOP_K, 0)
    inv = jnp.zeros((A,), jnp.int32).at[order].set(jnp.arange(A, dtype=jnp.int32))
    pos = inv + (pstarts - starts)[flat_e]
    return tile_expert, tile_valid, row_tok, pos.reshape(N, TOP_K)


def kernel(x, c, rel_bias, mod_attn_w, mod_attn_b, w_in, cmp_pos, cmp_w1, cmp_w2, diff_lambda,
           diff_norm_g, w_branch_a, w_branch_b, w_branch_c, w_out, ln1_g, ln1_b, mod_ffn_w,
           mod_ffn_b, router_w, router_b, exp_w_gu, exp_b_gu, exp_w_down, exp_b_down, ln2_g, ln2_b):
    Bn, S, D = x.shape
    L = w_in.shape[0]
    E = exp_w_gu.shape[1]
    T = T_ATT
    cdt = _MXU_DTYPE
    dn_alpha = (2 * L) ** 0.25
    assert S % T == 0 and S % TM_PROJ == 0 and (Bn * S * TOP_K) % TM_MOE == 0

    mod_w = jnp.stack([mod_attn_w, mod_ffn_w], axis=1).reshape(2 * L, D, 3 * D)
    mod_b = jnp.stack([mod_attn_b, mod_ffn_b], axis=1).reshape(2 * L, 3 * D)
    mods = _adaln(c, mod_w, mod_b).reshape(2 * L, Bn, 3, D)

    n_win = WINDOW // T + 1
    bias_a = _bias_tiles(rel_bias[:, :A_HEADS], T, 3).reshape(3, A_HEADS * T, T)
    tab_b = rel_bias[:, A_HEADS:A_HEADS + B_HEADS]
    bias_b = _bias_tiles(tab_b, T, 3).reshape(3, B_HEADS * T, T)
    bias_w = _bias_tiles(tab_b, T, n_win, WINDOW).reshape(n_win, B_HEADS * T, T)
    bias_c = _bias_tiles(rel_bias[:, A_HEADS + B_HEADS:], T, 3)
    ii = np.arange(T)
    cmask = jnp.asarray(np.stack([np.where(ii[:, None] >= ii[None, :], 0.0, -np.inf),
                                  np.zeros((T, T))]).astype(np.float32))

    n_cmp = (S - CMP_BLOCK) // CMP_STRIDE + 1
    n_slc = S // SLC_BLOCK
    n_sel = min(SLC_TOPN, n_slc)
    NC = S // CMP_STRIDE
    NSP = -(-n_slc // LANES) * LANES
    overlap = jnp.asarray(_slc_overlap_np(n_cmp, n_slc, NC, NSP)).astype(cdt)

    rows16 = CMP_STRIDE * 2 * HEAD_DIM

    for l in range(L):
        lam_init = 0.8 - 0.6 * math.exp(-0.3 * l)
        wl = w_in[l]
        wa = jnp.pad(wl[:, 0:544], ((0, 0), (0, 96))).astype(cdt)
        wb = wl[:, 548:1188].astype(cdt)
        wc = wl[:, 1200:2736].astype(cdt)
        ws = jnp.pad(jnp.concatenate([wl[:, 544:548], wl[:, 1188:1200]], axis=1),
                     ((0, 0), (0, LANES - 16))).astype(cdt)
        wg = wl[:, 2736:].astype(cdt)

        apack, bpack, cpack, side, mg = _inproj(x, mods[2 * l], wa, wb, wc, ws, wg)

        r2 = bpack[:, :, 256:384].reshape(Bn, NC, rows16)
        r2s = jnp.concatenate([r2[:, 1:], jnp.zeros_like(r2[:, :1])], axis=1)
        half = CMP_BLOCK // 2
        zpad = jnp.zeros((half, HEAD_DIM), F32)
        pa, pb, wca, wcb = [], [], [], []
        for j in range(2):
            pos = cmp_pos[l, j]
            w1 = cmp_w1[l, j].reshape(CMP_BLOCK, HEAD_DIM, CMP_HIDDEN)
            zw = jnp.zeros((half, HEAD_DIM, CMP_HIDDEN), F32)
            kv = (lambda a, z: jnp.concatenate([a, z], axis=1)) if j == 0 else \
                 (lambda a, z: jnp.concatenate([z, a], axis=1))
            pa.append(kv(pos[:half], zpad).reshape(1, rows16))
            pb.append(kv(pos[half:], zpad).reshape(1, rows16))
            wca.append(kv(w1[:half], zw).reshape(rows16, CMP_HIDDEN))
            wcb.append(kv(w1[half:], zw).reshape(rows16, CMP_HIDDEN))
        kvc = _compress(r2, r2s, jnp.stack(pa), jnp.stack(pb), jnp.stack(wca).astype(cdt),
                        jnp.stack(wcb).astype(cdt), cmp_w2[l].astype(cdt))

        oa = _dsa(apack, side, bias_a, cmask, T)
        ob = _nsa(bpack, kvc, side, bias_b, bias_w, overlap, T, n_cmp, n_sel)
        oc = _diff(cpack, bias_c, diff_lambda[l], diff_norm_g[l].reshape(1, C_VDIM), T, lam_init)

        rw = jnp.pad(router_w[l], ((0, 0), (0, LANES - E)))
        rb = jnp.pad(router_b[l], (0, LANES - E)).reshape(1, LANES)
        x1, h2, ti, tw = _outproj(
            oa, ob, oc, mg, x, mods[2 * l], mods[2 * l + 1],
            w_branch_a[l].astype(cdt), w_branch_b[l].astype(cdt), w_branch_c[l].astype(cdt),
            w_out[l].astype(cdt), jnp.stack([ln1_g[l], ln1_b[l]]), rw, rb, dn_alpha, E)

        N = Bn * S
        top_i = ti.reshape(N, LANES)[:, :TOP_K]
        tile_expert, tile_valid, row_tok, pos = _routing(top_i, E, TM_MOE)
        xs = jnp.take(h2.reshape(N, D), row_tok, axis=0)
        wgu = exp_w_gu[l].reshape(E, D, -1, 2)
        bgu = exp_b_gu[l].reshape(E, 1, -1, 2)
        ys = _moe(xs, tile_expert, tile_valid,
                  wgu[..., 0].astype(cdt), wgu[..., 1].astype(cdt), exp_w_down[l].astype(cdt),
                  bgu[..., 0], bgu[..., 1], exp_b_down[l].reshape(E, 1, D))
        yg = jnp.take(ys, pos.reshape(-1), axis=0).reshape(Bn, S, TOP_K * D)
        x = _combine(yg, tw, x1, mods[2 * l + 1], jnp.stack([ln2_g[l], ln2_b[l]]), dn_alpha)
    return x
```

```python
import functools
import math

import numpy as np
import jax
import jax.numpy as jnp
from jax import lax
from jax.experimental import pallas as pl
from jax.experimental.pallas import tpu as pltpu

F32 = jnp.float32
_MXU_DTYPE = jnp.bfloat16

HEAD_DIM = 64
A_HEADS = 4
IDX_HEADS = 4
IDX_DIM = 32
DSA_TOPK = 256
B_HEADS = 4
CMP_BLOCK = 32
CMP_STRIDE = 16
CMP_HIDDEN = 256
SLC_BLOCK = 64
SLC_TOPN = 16
WINDOW = 512
FORCE_SCORE = 1e9
C_HEADS = 4
C_VDIM = 2 * HEAD_DIM
N_BUCKETS = 32
MAX_DISTANCE = 128
TOP_K = 4
SWIGLU_LIMIT = 7.0
SWIGLU_ALPHA = 1.702
LN_EPS = 1e-5

QK_SCALE = HEAD_DIM ** -0.5
NEG = -1e30
INT_MIN = -2 ** 31
NEG_INF_KEY = INT_MIN + 0x7FFFFF

LANES = 128
SUBLANES = 8
VMEM_LIMIT_BYTES = 56 * 1024 * 1024

T_ATT = 256
TM_PROJ = 256
TM_MOE = 512


def _cparams(sem):
    return pltpu.CompilerParams(dimension_semantics=sem, vmem_limit_bytes=VMEM_LIMIT_BYTES)


def _dot(a, b):
    return jnp.dot(a, b, preferred_element_type=F32)


def _dot_nt(a, b):
    return lax.dot_general(a, b, (((1,), (1,)), ((), ())), preferred_element_type=F32)


def _adaln_kernel(c_ref, w_ref, b_ref, o_ref):
    c = c_ref[...]
    a = (c * jax.nn.sigmoid(c)).astype(_MXU_DTYPE)
    o_ref[0] = _dot(a, w_ref[0].astype(_MXU_DTYPE)) + b_ref[0]


def _adaln(c, w, b):
    G, D, N = w.shape
    Bn = c.shape[0]
    tn = 768
    return pl.pallas_call(
        _adaln_kernel,
        grid=(G, N // tn),
        in_specs=[
            pl.BlockSpec((Bn, D), lambda g, j: (0, 0)),
            pl.BlockSpec((1, D, tn), lambda g, j: (g, 0, j)),
            pl.BlockSpec((1, 1, tn), lambda g, j: (g, 0, j)),
        ],
        out_specs=pl.BlockSpec((1, Bn, tn), lambda g, j: (g, 0, j)),
        out_shape=jax.ShapeDtypeStruct((G, Bn, N), F32),
        compiler_params=_cparams(("parallel", "parallel")),
        name="adaln",
    )(c, w, b.reshape(G, 1, N))


def _inproj_kernel(x_ref, mod_ref, wa_ref, wb_ref, wc_ref, ws_ref, wg_ref,
                   oa_ref, ob_ref, oc_ref, os_ref, og_ref):
    x = x_ref[0]
    shift = mod_ref[0, 0:1, :]
    scale = mod_ref[0, 1:2, :]
    h = (x * (1.0 + scale) + shift).astype(_MXU_DTYPE)
    for w_ref, o_ref in ((wa_ref, oa_ref), (wb_ref, ob_ref), (wc_ref, oc_ref),
                         (ws_ref, os_ref), (wg_ref, og_ref)):
        n = w_ref.shape[1]
        for j in range(0, n, 512):
            w = min(512, n - j)
            o_ref[0, :, j:j + w] = _dot(h, w_ref[:, j:j + w]).astype(o_ref.dtype)


def _inproj(x, mod, wa, wb, wc, ws, wg):
    Bn, S, D = x.shape
    tm = TM_PROJ
    ws_ = [wa, wb, wc, ws, wg]
    dts = [_MXU_DTYPE, _MXU_DTYPE, _MXU_DTYPE, F32, _MXU_DTYPE]
    return pl.pallas_call(
        _inproj_kernel,
        grid=(Bn, S // tm),
        in_specs=[pl.BlockSpec((1, tm, D), lambda b, i: (b, i, 0)),
                  pl.BlockSpec((1, 3, D), lambda b, i: (b, 0, 0))]
                 + [pl.BlockSpec(w.shape, lambda b, i: (0, 0)) for w in ws_],
        out_specs=[pl.BlockSpec((1, tm, w.shape[1]), lambda b, i: (b, i, 0)) for w in ws_],
        out_shape=[jax.ShapeDtypeStruct((Bn, S, w.shape[1]), dt) for w, dt in zip(ws_, dts)],
        compiler_params=_cparams(("parallel", "parallel")),
        name="inproj",
    )(x, mod, *ws_)


def _compress_kernel(r_ref, rs_ref, pa_ref, pb_ref, wa_ref, wb_ref, w2_ref, o_ref):
    r = r_ref[0].astype(F32)
    rs = rs_ref[0].astype(F32)
    outs = []
    for j in range(2):
        xa = (r + pa_ref[j]).astype(_MXU_DTYPE)
        xb = (rs + pb_ref[j]).astype(_MXU_DTYPE)
        hid = jax.nn.gelu(_dot(xa, wa_ref[j]) + _dot(xb, wb_ref[j]))
        outs.append(_dot(hid.astype(_MXU_DTYPE), w2_ref[j]))
    o_ref[0] = jnp.concatenate(outs, axis=1).astype(o_ref.dtype)


def _compress(r2, r2s, pa, pb, wa, wb, w2):
    Bn, NC, KW = r2.shape
    return pl.pallas_call(
        _compress_kernel,
        grid=(Bn,),
        in_specs=[pl.BlockSpec((1, NC, KW), lambda b: (b, 0, 0)),
                  pl.BlockSpec((1, NC, KW), lambda b: (b, 0, 0)),
                  pl.BlockSpec(pa.shape, lambda b: (0, 0, 0)),
                  pl.BlockSpec(pb.shape, lambda b: (0, 0, 0)),
                  pl.BlockSpec(wa.shape, lambda b: (0, 0, 0)),
                  pl.BlockSpec(wb.shape, lambda b: (0, 0, 0)),
                  pl.BlockSpec(w2.shape, lambda b: (0, 0, 0))],
        out_specs=pl.BlockSpec((1, NC, 2 * HEAD_DIM), lambda b: (b, 0, 0)),
        out_shape=jax.ShapeDtypeStruct((Bn, NC, 2 * HEAD_DIM), _MXU_DTYPE),
        compiler_params=_cparams(("parallel",)),
        name="nsa_compress",
    )(r2, r2s, pa, pb, wa, wb, w2)


def _online_t(s, m, l, acc_ref, v_t):
    m_new = jnp.maximum(m, jnp.max(s, axis=0, keepdims=True))
    alpha = jnp.exp(m - m_new)
    p = jnp.exp(s - m_new)
    l = alpha * l + jnp.sum(p, axis=0, keepdims=True)
    acc_ref[...] = alpha * acc_ref[...] + _dot(v_t, p.astype(_MXU_DTYPE))
    return m_new, l


def _finish_t(m, l, acc):
    return jnp.where(m > 0.5 * NEG, acc / jnp.maximum(l, 1e-30), 0.0)


def _zero_extended_queries(q, n_heads, width, scale):
    t = q.shape[0]
    pad = jnp.zeros((t, LANES - width), q.dtype)
    rows = []
    for h in range(n_heads):
        qh = q[:, h * width:(h + 1) * width]
        if scale != 1.0:
            qh = qh * jnp.asarray(scale, q.dtype)
        rows.append(jnp.concatenate([qh, pad], axis=1))
    return jnp.concatenate(rows, axis=0)


def _heads_to_rows_t(o_t, n_heads):
    t = o_t.shape[1] // n_heads
    stacked = jnp.concatenate([o_t[:, h * t:(h + 1) * t] for h in range(n_heads)], axis=0)
    return stacked.T


def _dsa_kernel(aq_ref, akv_ref, vt_ref, iq_ref, ik_ref, side_ref, bias_ref, cmask_ref, o_ref,
                keys_ref, acc_ref, *, T, topk):
    qi = pl.program_id(1)
    nk = qi + 1
    H = A_HEADS

    q_z = _zero_extended_queries(aq_ref[0], H, HEAD_DIM, QK_SCALE)
    iq_z = _zero_extended_queries(iq_ref[0], IDX_HEADS, IDX_DIM, 1.0)
    iw_t = side_ref[0].T[0:IDX_HEADS] * (IDX_HEADS ** -0.5 * IDX_DIM ** -0.5)

    def score_chunk(kc, carry):
        start = pl.multiple_of(kc * T, T)
        sc = jnp.maximum(_dot_nt(ik_ref[0, pl.ds(start, T), :], iq_z), 0.0)
        isc = iw_t[0:1] * sc[:, 0:T]
        for h in range(1, IDX_HEADS):
            isc = isc + iw_t[h:h + 1] * sc[:, h * T:(h + 1) * T]
        isc = (isc + 0.0) + cmask_ref[jnp.minimum(qi - kc, 1)]
        bits = pltpu.bitcast(isc, jnp.int32)
        keys_ref[kc] = jnp.where(bits < 0, bits ^ 0x7FFFFFFF, bits)
        return carry

    lax.fori_loop(0, nk, score_chunk, 0)

    def count(pred):
        def body(kc, acc):
            hit = jnp.where(pred(keys_ref[kc]), 1.0, 0.0)
            return acc + jnp.sum(hit.reshape(T // SUBLANES, SUBLANES, T), axis=0)
        acc = lax.fori_loop(0, nk, body, jnp.zeros((SUBLANES, T), F32))
        return jnp.sum(acc, axis=0, keepdims=True)

    def bit_step(b, thr):
        cand = thr + lax.shift_left(jnp.int32(1), 31 - b)
        cnt = count(lambda k: k >= cand)
        return jnp.where(cnt >= float(topk), cand, thr)

    thr = lax.fori_loop(0, 32, bit_step, jnp.full((1, T), INT_MIN, jnp.int32))
    need = float(topk) - count(lambda k: k > thr)

    row = lax.broadcasted_iota(jnp.int32, (T, T), 0)
    col = lax.broadcasted_iota(jnp.int32, (T, T), 1)
    lower = jnp.where(col < row, 1.0, 0.0).astype(_MXU_DTYPE)

    acc_ref[...] = jnp.zeros(acc_ref.shape, F32)

    def attend(kc, carry):
        m, l, run = carry
        start = pl.multiple_of(kc * T, T)
        keys = keys_ref[kc]
        eq = keys == thr
        eqf = jnp.where(eq, 1.0, 0.0)
        prefix = _dot(lower, eqf.astype(_MXU_DTYPE)) + run
        sel = ((keys > thr) | (eq & (prefix < need))) & (keys != NEG_INF_KEY)
        run = run + jnp.sum(eqf, axis=0, keepdims=True)
        s = _dot_nt(akv_ref[0, pl.ds(start, T), :], q_z) + bias_ref[jnp.minimum(qi - kc, 2)]
        s = jnp.where(jnp.concatenate([sel] * H, axis=1), s, NEG)
        m, l = _online_t(s, m, l, acc_ref, vt_ref[0, kc])
        return m, l, run

    init = (jnp.full((1, H * T), NEG, F32), jnp.zeros((1, H * T), F32), jnp.zeros((1, T), F32))
    m, l, _ = lax.fori_loop(0, nk, attend, init)
    o_ref[0] = _heads_to_rows_t(_finish_t(m, l, acc_ref[...]), H).astype(o_ref.dtype)


def _dsa(apack, av_t, side, bias, cmask, T):
    Bn, S, _ = apack.shape
    topk = min(DSA_TOPK, S // 4)
    kern = functools.partial(_dsa_kernel, T=T, topk=topk)
    return pl.pallas_call(
        kern,
        grid=(Bn, S // T),
        in_specs=[pl.BlockSpec((1, T, 256), lambda b, i: (b, i, 0)),
                  pl.BlockSpec((1, S, 128), lambda b, i: (b, 0, 2)),
                  pl.BlockSpec((1,) + av_t.shape[1:], lambda b, i: (b, 0, 0, 0)),
                  pl.BlockSpec((1, T, 128), lambda b, i: (b, i, 3)),
                  pl.BlockSpec((1, S, 128), lambda b, i: (b, 0, 4)),
                  pl.BlockSpec((1, T, 128), lambda b, i: (b, i, 0)),
                  pl.BlockSpec(bias.shape, lambda b, i: (0, 0, 0)),
                  pl.BlockSpec(cmask.shape, lambda b, i: (0, 0, 0))],
        out_specs=pl.BlockSpec((1, T, 256), lambda b, i: (b, i, 0)),
        out_shape=jax.ShapeDtypeStruct((Bn, S, 256), _MXU_DTYPE),
        scratch_shapes=[pltpu.VMEM((S // T, T, T), jnp.int32),
                        pltpu.VMEM((HEAD_DIM, A_HEADS * T), F32)],
        compiler_params=_cparams(("parallel", "arbitrary")),
        name="dsa_attention",
    )(apack, apack, av_t, apack, apack, side, bias, cmask)


def _nsa_kernel(bq_ref, kvc_ref, vct_ref, kvs_ref, vst_ref, kvw_ref, vwt_ref, side_ref,
                bias_ref, wbias_ref, ovt_ref, o_ref, acc_ref, *, T, n_cmp, n_sel, n_win):
    qi = pl.program_id(1)
    nk = qi + 1
    H = B_HEADS
    NC = kvc_ref.shape[1]
    NS = ovt_ref.shape[0]
    log2_blk = int(math.log2(SLC_BLOCK))

    q_z = _zero_extended_queries(bq_ref[0], H, HEAD_DIM, QK_SCALE)
    t_q = qi * T + lax.broadcasted_iota(jnp.int32, (1, T), 1)
    t_st = jnp.concatenate([t_q] * H, axis=1)

    n_idx = lax.broadcasted_iota(jnp.int32, (NC, 1), 0)
    cvalid = (n_idx * CMP_STRIDE + (CMP_BLOCK - 1) <= t_st) & (n_idx < n_cmp)
    lc = jnp.where(cvalid, _dot_nt(kvc_ref[0], q_z), NEG)
    mc = jnp.max(lc, axis=0, keepdims=True)
    pc = jnp.where(cvalid, jnp.exp(lc - mc), 0.0)
    pc = pc / jnp.maximum(jnp.sum(pc, axis=0, keepdims=True), 1e-30)
    o_cmp = _dot(vct_ref[0], pc.astype(_MXU_DTYPE))

    psum = pc[:, 0:T]
    for h in range(1, H):
        psum = psum + pc[:, h * T:(h + 1) * T]
    p_hi = psum.astype(_MXU_DTYPE)
    p_lo = (psum - p_hi.astype(F32)).astype(_MXU_DTYPE)
    imp = _dot(ovt_ref[...], p_hi) + _dot(ovt_ref[...], p_lo)

    blk = lax.broadcasted_iota(jnp.int32, (NS, T), 0)
    blkf = blk.astype(F32)
    cur = lax.shift_right_logical(t_q, log2_blk)
    forced = (blk == 0) | (blk == cur) | (blk == cur - 1)
    val = jnp.where(blk <= cur, jnp.where(forced, FORCE_SCORE, imp), -jnp.inf)

    def pick_one(_, carry):
        val, sel = carry
        mx = jnp.max(val, axis=0, keepdims=True)
        first = jnp.min(jnp.where(val == mx, blkf, float(NS)), axis=0, keepdims=True)
        pick = blkf == first
        sel = jnp.where(pick & (mx > -jnp.inf), 1.0, sel)
        return jnp.where(pick, -jnp.inf, val), sel

    _, sel = lax.fori_loop(0, n_sel, pick_one, (val, jnp.zeros((NS, T), F32)))
    sel = sel.astype(_MXU_DTYPE)

    e_row = lax.shift_right_logical(lax.broadcasted_iota(jnp.int32, (T, NS), 0), log2_blk)
    e_col = lax.broadcasted_iota(jnp.int32, (T, NS), 1)
    bpc = T // SLC_BLOCK

    init = (jnp.full((1, H * T), NEG, F32), jnp.zeros((1, H * T), F32))

    acc_ref[...] = jnp.zeros(acc_ref.shape, F32)

    def slc_chunk(kc, carry):
        start = pl.multiple_of(kc * T, T)
        expand = jnp.where(e_col == kc * bpc + e_row, 1.0, 0.0).astype(_MXU_DTYPE)
        addm = jnp.where(_dot(expand, sel) > 0.5, 0.0, NEG)
        s = (_dot_nt(kvs_ref[0, pl.ds(start, T), :], q_z) + bias_ref[jnp.minimum(qi - kc, 2)]
             + jnp.concatenate([addm] * H, axis=1))
        return _online_t(s, *carry, acc_ref, vst_ref[0, kc])

    m, l = lax.fori_loop(0, nk, slc_chunk, init)
    o_slc = _finish_t(m, l, acc_ref[...])

    acc_ref[...] = jnp.zeros(acc_ref.shape, F32)
    carry = init
    for delta in range(n_win):
        kc = qi - delta
        kcc = jnp.maximum(kc, 0)
        start = pl.multiple_of(kcc * T, T)
        s = (_dot_nt(kvw_ref[0, pl.ds(start, T), :], q_z) + wbias_ref[delta]
             + jnp.where(kc >= 0, 0.0, NEG))
        carry = _online_t(s, *carry, acc_ref, vwt_ref[0, kcc])
    o_win = _finish_t(*carry, acc_ref[...])

    g = jax.nn.sigmoid(side_ref[0].T[IDX_HEADS:IDX_HEADS + 3 * H])
    outs = []
    for h in range(H):
        sl = slice(h * T, (h + 1) * T)
        outs.append(g[3 * h:3 * h + 1] * o_cmp[:, sl] + g[3 * h + 1:3 * h + 2] * o_slc[:, sl]
                    + g[3 * h + 2:3 * h + 3] * o_win[:, sl])
    o_ref[0] = jnp.concatenate(outs, axis=0).T.astype(o_ref.dtype)


def _nsa(bpack, kvc, vc_t, vs_t, vw_t, side, bias, wbias, overlap_t, T, n_cmp, n_sel):
    Bn, S, _ = bpack.shape
    NC = kvc.shape[1]
    kern = functools.partial(_nsa_kernel, T=T, n_cmp=n_cmp, n_sel=n_sel, n_win=wbias.shape[0])
    vtspec = lambda a: pl.BlockSpec((1,) + a.shape[1:], lambda b, i: (b, 0, 0, 0))
    return pl.pallas_call(
        kern,
        grid=(Bn, S // T),
        in_specs=[pl.BlockSpec((1, T, 256), lambda b, i: (b, i, 0)),
                  pl.BlockSpec((1, NC, 128), lambda b, i: (b, 0, 0)),
                  pl.BlockSpec((1, HEAD_DIM, NC), lambda b, i: (b, 0, 0)),
                  pl.BlockSpec((1, S, 128), lambda b, i: (b, 0, 3)),
                  vtspec(vs_t),
                  pl.BlockSpec((1, S, 128), lambda b, i: (b, 0, 4)),
                  vtspec(vw_t),
                  pl.BlockSpec((1, T, 128), lambda b, i: (b, i, 0)),
                  pl.BlockSpec(bias.shape, lambda b, i: (0, 0, 0)),
                  pl.BlockSpec(wbias.shape, lambda b, i: (0, 0, 0)),
                  pl.BlockSpec(overlap_t.shape, lambda b, i: (0, 0))],
        out_specs=pl.BlockSpec((1, T, 256), lambda b, i: (b, i, 0)),
        out_shape=jax.ShapeDtypeStruct((Bn, S, 256), _MXU_DTYPE),
        scratch_shapes=[pltpu.VMEM((HEAD_DIM, B_HEADS * T), F32)],
        compiler_params=_cparams(("parallel", "arbitrary")),
        name="nsa_attention",
    )(bpack, kvc, vc_t, bpack, vs_t, bpack, vw_t, side, bias, wbias, overlap_t)


def _diff_kernel(cq_ref, ck_ref, vt_ref, bias_ref, dl_ref, g_ref, o_ref, acc_ref, *, T, lam_init):
    qi = pl.program_id(1)
    nk = qi + 1
    H = C_HEADS
    dl = dl_ref[...]
    lam = (jnp.exp(jnp.sum(dl[0:1] * dl[1:2], axis=1, keepdims=True))
           - jnp.exp(jnp.sum(dl[2:3] * dl[3:4], axis=1, keepdims=True)) + lam_init)

    cq = cq_ref[0]
    lane = lax.broadcasted_iota(jnp.int32, (T, 2 * HEAD_DIM), 1)
    q_z = []
    for h in range(H):
        qh = cq[:, h * 2 * HEAD_DIM:(h + 1) * 2 * HEAD_DIM] * jnp.asarray(QK_SCALE, cq.dtype)
        zero = jnp.zeros_like(qh)
        q_z.append(jnp.concatenate([jnp.where(lane < HEAD_DIM, qh, zero),
                                    jnp.where(lane >= HEAD_DIM, qh, zero)], axis=0))

    acc_ref[...] = jnp.zeros(acc_ref.shape, F32)

    def chunk(kc, carry):
        start = pl.multiple_of(kc * T, T)
        dsel = jnp.minimum(qi - kc, 2)
        kch = ck_ref[0, pl.ds(start, T), :]
        v_t = vt_ref[0, kc]
        out = []
        for h in range(H):
            b = bias_ref[dsel, h]
            s = (_dot_nt(kch[:, h * 2 * HEAD_DIM:(h + 1) * 2 * HEAD_DIM], q_z[h])
                 + jnp.concatenate([b, b], axis=1))
            out.append(_online_t(s, *carry[h], acc_ref.at[h], v_t[h * C_VDIM:(h + 1) * C_VDIM]))
        return tuple(out)

    init = tuple((jnp.full((1, 2 * T), NEG, F32), jnp.zeros((1, 2 * T), F32)) for _ in range(H))
    stats = lax.fori_loop(0, nk, chunk, init)

    outs = []
    for h in range(H):
        o = _finish_t(*stats[h], acc_ref[h])
        oc = o[:, 0:T] - lam * o[:, T:2 * T]
        rms = lax.rsqrt(jnp.mean(jnp.square(oc), axis=0, keepdims=True) + LN_EPS)
        outs.append(oc * rms * g_ref[...] * (1.0 - lam_init))
    o_ref[0] = jnp.concatenate(outs, axis=0).T.astype(o_ref.dtype)


def _diff(cpack, cv_t, bias, dl, g, T, lam_init):
    Bn, S, _ = cpack.shape
    W = C_HEADS * C_VDIM
    kern = functools.partial(_diff_kernel, T=T, lam_init=lam_init)
    return pl.pallas_call(
        kern,
        grid=(Bn, S // T),
        in_specs=[pl.BlockSpec((1, T, W), lambda b, i: (b, i, 0)),
                  pl.BlockSpec((1, S, W), lambda b, i: (b, 0, 1)),
                  pl.BlockSpec((1,) + cv_t.shape[1:], lambda b, i: (b, 0, 0, 0)),
                  pl.BlockSpec(bias.shape, lambda b, i: (0, 0, 0, 0)),
                  pl.BlockSpec(dl.shape, lambda b, i: (0, 0)),
                  pl.BlockSpec(g.shape, lambda b, i: (0, 0))],
        out_specs=pl.BlockSpec((1, T, W), lambda b, i: (b, i, 0)),
        out_shape=jax.ShapeDtypeStruct((Bn, S, W), _MXU_DTYPE),
        scratch_shapes=[pltpu.VMEM((C_HEADS, C_VDIM, 2 * T), F32)],
        compiler_params=_cparams(("parallel", "arbitrary")),
        name="diff_attention",
    )(cpack, cpack, cv_t, bias, dl, g)


def _layer_norm(v, g, b):
    mu = jnp.mean(v, axis=1, keepdims=True)
    d = v - mu
    var = jnp.mean(jnp.square(d), axis=1, keepdims=True)
    return d * lax.rsqrt(var + LN_EPS) * g + b


def _split_hi_lo(v):
    hi = v.astype(_MXU_DTYPE)
    lo = (v - hi.astype(F32)).astype(_MXU_DTYPE)
    return hi, lo


def _outproj_kernel(oa_ref, ob_ref, oc_ref, mg_ref, x_ref, mod1_ref, mod2_ref,
                    wa_ref, wb_ref, wc_ref, wo_ref, ln_ref, rw_ref, rb_ref,
                    x1_ref, h2_ref, tr_ref, tw_ref, cnt_ref, run_ref, *, dn_alpha, n_experts):
    D = x_ref.shape[2]
    tm = x_ref.shape[1]
    first_step = (pl.program_id(0) == 0) & (pl.program_id(1) == 0)

    @pl.when(first_step)
    def _():
        run_ref[...] = jnp.zeros(run_ref.shape, F32)

    ya = _dot(oa_ref[0], wa_ref[...])
    yb = _dot(ob_ref[0], wb_ref[...])
    yc = _dot(oc_ref[0], wc_ref[...])
    mg = mg_ref[0].astype(F32)
    merged = (jax.nn.sigmoid(mg[:, 0:D]) * ya + jax.nn.sigmoid(mg[:, D:2 * D]) * yb
              + jax.nn.sigmoid(mg[:, 2 * D:3 * D]) * yc)
    y = _dot(merged.astype(_MXU_DTYPE), wo_ref[...])
    gate1 = mod1_ref[0, 2:3, :]
    x1 = _layer_norm(dn_alpha * x_ref[0] + (1.0 + gate1) * y, ln_ref[0:1, :], ln_ref[1:2, :])
    x1_ref[0] = x1
    h2 = x1 * (1.0 + mod2_ref[0, 1:2, :]) + mod2_ref[0, 0:1, :]
    h2_ref[0] = h2.astype(h2_ref.dtype)

    h_hi, h_lo = _split_hi_lo(h2)
    w_hi, w_lo = _split_hi_lo(rw_ref[...])
    logits = _dot(h_hi, w_hi) + _dot(h_hi, w_lo) + _dot(h_lo, w_hi) + rb_ref[...]
    lane = lax.broadcasted_iota(jnp.int32, logits.shape, 1)
    lanef = lane.astype(F32)
    work = jnp.where(lane < n_experts, logits, -jnp.inf)
    tv = jnp.zeros(logits.shape, F32)
    firsts, onehots = [], []
    v0 = None
    for k in range(TOP_K):
        mx = jnp.max(work, axis=1, keepdims=True)
        first = jnp.min(jnp.where(work == mx, lanef, float(LANES)), axis=1, keepdims=True)
        if k == 0:
            v0 = mx
        tv = jnp.where(lane == k, jnp.exp(mx - v0), tv)
        hit = lanef == first
        work = jnp.where(hit, -jnp.inf, work)
        firsts.append(first)
        onehots.append(jnp.where(hit, 1.0, 0.0))
    tw_ref[0] = tv / jnp.sum(tv, axis=1, keepdims=True)

    chosen = onehots[0] + onehots[1] + onehots[2] + onehots[3]
    r_i = lax.broadcasted_iota(jnp.int32, (tm, tm), 0)
    c_i = lax.broadcasted_iota(jnp.int32, (tm, tm), 1)
    earlier = jnp.where(c_i < r_i, 1.0, 0.0).astype(_MXU_DTYPE)
    base = _dot(earlier, chosen.astype(_MXU_DTYPE)) + run_ref[0:1, :]
    tr = jnp.zeros(logits.shape, F32)
    for k in range(TOP_K):
        rank = jnp.sum(onehots[k] * base, axis=1, keepdims=True)
        tr = jnp.where(lane == k, firsts[k], tr)
        tr = jnp.where(lane == TOP_K + k, rank, tr)
    tr_ref[0] = tr.astype(jnp.int32)
    run_ref[...] = run_ref[...] + jnp.sum(chosen, axis=0, keepdims=True)
    cnt_ref[...] = run_ref[...]


def _outproj(oa, ob, oc, mg, x, mod1, mod2, wa, wb, wc, wo, ln, rw, rb, dn_alpha, n_experts):
    Bn, S, D = x.shape
    tm = TM_PROJ
    row = lambda w: pl.BlockSpec((1, tm, w), lambda b, i: (b, i, 0))
    full = lambda a: pl.BlockSpec(a.shape, lambda b, i: (0,) * a.ndim)
    modspec = pl.BlockSpec((1, 3, D), lambda b, i: (b, 0, 0))
    kern = functools.partial(_outproj_kernel, dn_alpha=dn_alpha, n_experts=n_experts)
    return pl.pallas_call(
        kern,
        grid=(Bn, S // tm),
        in_specs=[row(oa.shape[2]), row(ob.shape[2]), row(oc.shape[2]), row(mg.shape[2]), row(D),
                  modspec, modspec, full(wa), full(wb), full(wc), full(wo), full(ln), full(rw),
                  full(rb)],
        out_specs=[row(D), row(D), row(LANES), row(LANES),
                   pl.BlockSpec((SUBLANES, LANES), lambda b, i: (0, 0))],
        out_shape=[jax.ShapeDtypeStruct((Bn, S, D), F32),
                   jax.ShapeDtypeStruct((Bn, S, D), _MXU_DTYPE),
                   jax.ShapeDtypeStruct((Bn, S, LANES), jnp.int32),
                   jax.ShapeDtypeStruct((Bn, S, LANES), F32),
                   jax.ShapeDtypeStruct((SUBLANES, LANES), F32)],
        scratch_shapes=[pltpu.VMEM((SUBLANES, LANES), F32)],
        compiler_params=_cparams(("arbitrary", "arbitrary")),
        name="merge_outproj_ln_router",
    )(oa, ob, oc, mg, x, mod1, mod2, wa, wb, wc, wo, ln, rw, rb)


def _deinterleave_kernel(w_ref, p_ref, og_ref, ou_ref):
    n = w_ref.shape[1]
    for j in range(n // (2 * LANES)):
        blk = w_ref[:, j * 2 * LANES:(j + 1) * 2 * LANES].astype(_MXU_DTYPE)
        r = _dot(blk, p_ref[...])
        og_ref[:, j * LANES:(j + 1) * LANES] = r[:, 0:LANES].astype(og_ref.dtype)
        ou_ref[:, j * LANES:(j + 1) * LANES] = r[:, LANES:2 * LANES].astype(ou_ref.dtype)


def _deinterleave(w):
    R, N2 = w.shape
    tr = 512
    perm = np.zeros((2 * LANES, 2 * LANES), np.float32)
    perm[2 * np.arange(LANES), np.arange(LANES)] = 1.0
    perm[2 * np.arange(LANES) + 1, LANES + np.arange(LANES)] = 1.0
    return pl.pallas_call(
        _deinterleave_kernel,
        grid=(R // tr,),
        in_specs=[pl.BlockSpec((tr, N2), lambda i: (i, 0)),
                  pl.BlockSpec(perm.shape, lambda i: (0, 0))],
        out_specs=[pl.BlockSpec((tr, N2 // 2), lambda i: (i, 0))] * 2,
        out_shape=[jax.ShapeDtypeStruct((R, N2 // 2), _MXU_DTYPE)] * 2,
        compiler_params=_cparams(("parallel",)),
        name="expert_weight_deinterleave",
    )(w, jnp.asarray(perm).astype(_MXU_DTYPE))


def _moe_kernel(te_ref, tv_ref, x_ref, wg_ref, wu_ref, wd_ref, bg_ref, bu_ref, bd_ref, o_ref):
    i = pl.program_id(0)

    @pl.when(tv_ref[i] > 0)
    def _():
        x = x_ref[...]
        dff = wg_ref.shape[2]
        ck = 512
        y = jnp.zeros(o_ref.shape, F32)
        for j in range(0, dff, ck):
            gate = jnp.minimum(_dot(x, wg_ref[0, :, j:j + ck]) + bg_ref[0, :, j:j + ck], SWIGLU_LIMIT)
            up = jnp.clip(_dot(x, wu_ref[0, :, j:j + ck]) + bu_ref[0, :, j:j + ck],
                          -SWIGLU_LIMIT, SWIGLU_LIMIT)
            act = (up + 1.0) * (gate * jax.nn.sigmoid(SWIGLU_ALPHA * gate))
            y = y + _dot(act.astype(_MXU_DTYPE), wd_ref[0, j:j + ck, :])
        o_ref[...] = (y + bd_ref[0]).astype(o_ref.dtype)


def _moe(xs, tile_expert, tile_valid, wg, wu, wd, bg, bu, bd):
    R, D = xs.shape
    tm = TM_MOE
    nt = R // tm
    wspec = lambda a: pl.BlockSpec((1,) + a.shape[1:], lambda i, te, tv: (te[i], 0, 0))
    grid_spec = pltpu.PrefetchScalarGridSpec(
        num_scalar_prefetch=2,
        grid=(nt,),
        in_specs=[pl.BlockSpec((tm, D), lambda i, te, tv: (i, 0)),
                  wspec(wg), wspec(wu), wspec(wd), wspec(bg), wspec(bu), wspec(bd)],
        out_specs=pl.BlockSpec((tm, D), lambda i, te, tv: (i, 0)),
    )
    return pl.pallas_call(
        _moe_kernel,
        grid_spec=grid_spec,
        out_shape=jax.ShapeDtypeStruct((R, D), _MXU_DTYPE),
        compiler_params=_cparams(("arbitrary",)),
        name="moe_grouped_mlp",
    )(tile_expert, tile_valid, xs, wg, wu, wd, bg, bu, bd)


def _combine_kernel(yg_ref, tw_ref, x_ref, mod_ref, ln_ref, o_ref, *, dn_alpha):
    D = x_ref.shape[2]
    tw = tw_ref[0]
    y = tw[:, 0:1] * yg_ref[0, :, 0:D].astype(F32)
    for k in range(1, TOP_K):
        y = y + tw[:, k:k + 1] * yg_ref[0, :, k * D:(k + 1) * D].astype(F32)
    gate = mod_ref[0, 2:3, :]
    o_ref[0] = _layer_norm(dn_alpha * x_ref[0] + (1.0 + gate) * y, ln_ref[0:1, :], ln_ref[1:2, :])


def _combine(yg, tw, x, mod, ln, dn_alpha):
    Bn, S, D = x.shape
    tm = TM_PROJ
    row = lambda w: pl.BlockSpec((1, tm, w), lambda b, i: (b, i, 0))
    return pl.pallas_call(
        functools.partial(_combine_kernel, dn_alpha=dn_alpha),
        grid=(Bn, S // tm),
        in_specs=[row(TOP_K * D), row(LANES), row(D),
                  pl.BlockSpec((1, 3, D), lambda b, i: (b, 0, 0)),
                  pl.BlockSpec(ln.shape, lambda b, i: (0, 0))],
        out_specs=row(D),
        out_shape=jax.ShapeDtypeStruct((Bn, S, D), F32),
        compiler_params=_cparams(("parallel", "parallel")),
        name="moe_combine_ln",
    )(yg, tw, x, mod, ln)


def _bucket_np(dist):
    exact = N_BUCKETS // 2
    n = np.maximum(dist, 0)
    nf = np.maximum(n, 1).astype(np.float32)
    large = exact + (np.log(nf / exact) / math.log(MAX_DISTANCE / exact)
                     * (N_BUCKETS - exact)).astype(np.int32)
    large = np.minimum(large, N_BUCKETS - 1)
    return np.where(n < exact, n, large)


def _bias_tiles_t(tab, T, n_delta, window=None):
    j = np.arange(T)[:, None]
    i = np.arange(T)[None, :]
    d = np.stack([i - j + delta * T for delta in range(n_delta)])
    ok = d >= 0 if window is None else (d >= 0) & (d < window)
    bucket = jnp.asarray(np.where(ok, _bucket_np(d), -1).astype(np.int32))[:, None]
    tiles = jnp.full((n_delta, tab.shape[1], T, T), NEG, F32)
    for b in range(N_BUCKETS):
        tiles = jnp.where(bucket == b, tab[b][None, :, None, None], tiles)
    return tiles


def _stack_tiles(tiles):
    n, H, T, _ = tiles.shape
    return jnp.transpose(tiles, (0, 2, 1, 3)).reshape(n, T, H * T)


def _slc_overlap_np(n_cmp, n_slc, rows):
    start = np.arange(n_cmp) * CMP_STRIDE
    end = start + CMP_BLOCK
    bs = np.arange(n_slc) * SLC_BLOCK
    ov = (start[:, None] < bs[None, :] + SLC_BLOCK) & (end[:, None] > bs[None, :])
    out = np.zeros((rows, n_slc), np.float32)
    out[:n_cmp] = ov
    return out


def _chunked_t(v, T):
    Bn, S, d = v.shape
    return jnp.transpose(v.reshape(Bn, S // T, T, d), (0, 1, 3, 2))


def _routing(top_i, rank, counts, tm):
    N = top_i.shape[0]
    E = counts.shape[0]
    A = N * TOP_K
    nt = A // tm + E
    padded = ((counts + tm - 1) // tm) * tm
    starts = jnp.cumsum(counts) - counts
    pstarts = jnp.cumsum(padded) - padded
    pend = pstarts + padded
    tile_start = jnp.arange(nt, dtype=jnp.int32) * tm
    tile_valid = (tile_start < pend[-1]).astype(jnp.int32)
    te = jnp.minimum(jnp.sum((tile_start[:, None] >= pend[None, :]).astype(jnp.int32), axis=1), E - 1)
    last_valid = jnp.max(jnp.where(tile_valid > 0, te, 0))
    tile_expert = jnp.where(tile_valid > 0, te, last_valid).astype(jnp.int32)
    onehot_t = tile_expert[:, None] == jnp.arange(E)[None, :]
    pick = lambda tab: jnp.sum(jnp.where(onehot_t, tab[None, :], 0), axis=1)
    order = jnp.argsort(top_i.reshape(A), stable=True).astype(jnp.int32)
    order = jnp.concatenate([order, jnp.zeros((tm,), jnp.int32)])
    in_group = tile_start - pick(pstarts)
    base = jnp.clip(pick(starts) + in_group, 0, A)
    win = jax.vmap(lambda b: lax.dynamic_slice(order, (b,), (tm,)))(base)
    ok = ((in_group[:, None] + jnp.arange(tm)[None, :]) < pick(counts)[:, None]) \
        & (tile_valid[:, None] > 0)
    row_tok = jnp.where(ok, win // TOP_K, 0).reshape(nt * tm)
    onehot_a = top_i[:, :, None] == jnp.arange(E)[None, None, :]
    pos = rank + jnp.sum(jnp.where(onehot_a, pstarts[None, None, :], 0), axis=2)
    return tile_expert, tile_valid, row_tok, pos.astype(jnp.int32)


def kernel(x, c, rel_bias, mod_attn_w, mod_attn_b, w_in, cmp_pos, cmp_w1, cmp_w2, diff_lambda,
           diff_norm_g, w_branch_a, w_branch_b, w_branch_c, w_out, ln1_g, ln1_b, mod_ffn_w,
           mod_ffn_b, router_w, router_b, exp_w_gu, exp_b_gu, exp_w_down, exp_b_down, ln2_g, ln2_b):
    Bn, S, D = x.shape
    L = w_in.shape[0]
    E = exp_w_gu.shape[1]
    T = T_ATT
    cdt = _MXU_DTYPE
    dn_alpha = (2 * L) ** 0.25
    assert S % T == 0 and S % TM_PROJ == 0 and (Bn * S * TOP_K) % TM_MOE == 0
    assert (S // SLC_BLOCK) % SUBLANES == 0 and E <= LANES

    mod_w = jnp.stack([mod_attn_w, mod_ffn_w], axis=1).reshape(2 * L, D, 3 * D)
    mod_b = jnp.stack([mod_attn_b, mod_ffn_b], axis=1).reshape(2 * L, 3 * D)
    mods = _adaln(c, mod_w, mod_b).reshape(2 * L, Bn, 3, D)

    n_win = WINDOW // T + 1
    tab_b = rel_bias[:, A_HEADS:A_HEADS + B_HEADS]
    bias_a = _stack_tiles(_bias_tiles_t(rel_bias[:, :A_HEADS], T, 3))
    bias_b = _stack_tiles(_bias_tiles_t(tab_b, T, 3))
    bias_w = _stack_tiles(_bias_tiles_t(tab_b, T, n_win, WINDOW))
    bias_c = _bias_tiles_t(rel_bias[:, A_HEADS + B_HEADS:], T, 3)
    jj = np.arange(T)
    cmask = jnp.asarray(np.stack([np.where(jj[None, :] >= jj[:, None], 0.0, -np.inf),
                                  np.zeros((T, T))]).astype(np.float32))

    n_cmp = (S - CMP_BLOCK) // CMP_STRIDE + 1
    n_slc = S // SLC_BLOCK
    n_sel = min(SLC_TOPN, n_slc)
    NC = S // CMP_STRIDE
    overlap_t = jnp.asarray(_slc_overlap_np(n_cmp, n_slc, NC).T).astype(cdt)

    rows16 = CMP_STRIDE * 2 * HEAD_DIM

    for l in range(L):
        lam_init = 0.8 - 0.6 * math.exp(-0.3 * l)
        wl = w_in[l]
        wa = jnp.pad(wl[:, 0:544], ((0, 0), (0, 96))).astype(cdt)
        wb = wl[:, 548:1188].astype(cdt)
        wc = wl[:, 1200:2736].astype(cdt)
        ws = jnp.pad(jnp.concatenate([wl[:, 544:548], wl[:, 1188:1200]], axis=1),
                     ((0, 0), (0, LANES - 16))).astype(cdt)
        wg = wl[:, 2736:].astype(cdt)

        apack, bpack, cpack, side, mg = _inproj(x, mods[2 * l], wa, wb, wc, ws, wg)

        r2 = bpack[:, :, 256:384].reshape(Bn, NC, rows16)
        r2s = jnp.concatenate([r2[:, 1:], jnp.zeros_like(r2[:, :1])], axis=1)
        half = CMP_BLOCK // 2
        zpad = jnp.zeros((half, HEAD_DIM), F32)
        pa, pb, wca, wcb = [], [], [], []
        for j in range(2):
            pos = cmp_pos[l, j]
            w1 = cmp_w1[l, j].reshape(CMP_BLOCK, HEAD_DIM, CMP_HIDDEN)
            zw = jnp.zeros((half, HEAD_DIM, CMP_HIDDEN), F32)
            kv = (lambda a, z: jnp.concatenate([a, z], axis=1)) if j == 0 else \
                 (lambda a, z: jnp.concatenate([z, a], axis=1))
            pa.append(kv(pos[:half], zpad).reshape(1, rows16))
            pb.append(kv(pos[half:], zpad).reshape(1, rows16))
            wca.append(kv(w1[:half], zw).reshape(rows16, CMP_HIDDEN))
            wcb.append(kv(w1[half:], zw).reshape(rows16, CMP_HIDDEN))
        kvc = _compress(r2, r2s, jnp.stack(pa), jnp.stack(pb), jnp.stack(wca).astype(cdt),
                        jnp.stack(wcb).astype(cdt), cmp_w2[l].astype(cdt))

        oa = _dsa(apack, _chunked_t(apack[:, :, 320:384], T), side, bias_a, cmask, T)
        ob = _nsa(bpack, kvc, jnp.transpose(kvc[:, :, HEAD_DIM:], (0, 2, 1)),
                  _chunked_t(bpack[:, :, 448:512], T), _chunked_t(bpack[:, :, 576:640], T),
                  side, bias_b, bias_w, overlap_t, T, n_cmp, n_sel)
        oc = _diff(cpack, _chunked_t(cpack[:, :, 1024:1536], T), bias_c, diff_lambda[l],
                   diff_norm_g[l].reshape(C_VDIM, 1), T, lam_init)

        rw = jnp.pad(router_w[l], ((0, 0), (0, LANES - E)))
        rb = jnp.pad(router_b[l], (0, LANES - E)).reshape(1, LANES)
        x1, h2, tr, tw, cnt = _outproj(
            oa, ob, oc, mg, x, mods[2 * l], mods[2 * l + 1],
            w_branch_a[l].astype(cdt), w_branch_b[l].astype(cdt), w_branch_c[l].astype(cdt),
            w_out[l].astype(cdt), jnp.stack([ln1_g[l], ln1_b[l]]), rw, rb, dn_alpha, E)

        N = Bn * S
        tr = tr.reshape(N, LANES)
        tile_expert, tile_valid, row_tok, pos = _routing(
            tr[:, 0:TOP_K], tr[:, TOP_K:2 * TOP_K], cnt[0, :E].astype(jnp.int32), TM_MOE)
        xs = jnp.take(h2.reshape(N, D), row_tok, axis=0)
        wgate, wup = _deinterleave(exp_w_gu[l].reshape(E * D, -1))
        bgu = exp_b_gu[l].reshape(E, 1, -1, 2)
        ys = _moe(xs, tile_expert, tile_valid,
                  wgate.reshape(E, D, -1), wup.reshape(E, D, -1), exp_w_down[l].astype(cdt),
                  bgu[..., 0], bgu[..., 1], exp_b_down[l].reshape(E, 1, D))
        yg = jnp.take(ys, pos.reshape(-1), axis=0).reshape(Bn, S, TOP_K * D)
        x = _combine(yg, tw, x1, mods[2 * l + 1], jnp.stack([ln2_g[l], ln2_b[l]]), dn_alpha)
    return x
```

```python
import functools
import math

import numpy as np
import jax
import jax.numpy as jnp
from jax import lax
from jax.experimental import pallas as pl
from jax.experimental.pallas import tpu as pltpu

F32 = jnp.float32
_MXU_DTYPE = jnp.bfloat16

HEAD_DIM = 64
A_HEADS = 4
IDX_HEADS = 4
IDX_DIM = 32
DSA_TOPK = 256
B_HEADS = 4
CMP_BLOCK = 32
CMP_STRIDE = 16
CMP_HIDDEN = 256
SLC_BLOCK = 64
SLC_TOPN = 16
WINDOW = 512
FORCE_SCORE = 1e9
C_HEADS = 4
C_VDIM = 2 * HEAD_DIM
N_BUCKETS = 32
MAX_DISTANCE = 128
TOP_K = 4
SWIGLU_LIMIT = 7.0
SWIGLU_ALPHA = 1.702
LN_EPS = 1e-5

LOG2E = math.log2(math.e)
Q_FOLD = HEAD_DIM ** -0.5 * LOG2E
NEG = -1e30
INT_MIN = -2 ** 31
NEG_INF_KEY = INT_MIN + 0x7FFFFF

LANES = 128
SUBLANES = 8
VMEM_LIMIT_BYTES = 56 * 1024 * 1024

T_ATT = 256
TM_PROJ = 256
TM_MOE = 512


def _cparams(sem):
    return pltpu.CompilerParams(dimension_semantics=sem, vmem_limit_bytes=VMEM_LIMIT_BYTES)


def _dot(a, b):
    return jnp.dot(a, b, preferred_element_type=F32)


def _dot_nt(a, b):
    return lax.dot_general(a, b, (((1,), (1,)), ((), ())), preferred_element_type=F32)


def _adaln_kernel(c_ref, w_ref, b_ref, o_ref):
    c = c_ref[...]
    a = (c * jax.nn.sigmoid(c)).astype(_MXU_DTYPE)
    o_ref[0] = _dot(a, w_ref[0].astype(_MXU_DTYPE)) + b_ref[0]


def _adaln(c, w, b):
    G, D, N = w.shape
    Bn = c.shape[0]
    tn = 768
    return pl.pallas_call(
        _adaln_kernel,
        grid=(G, N // tn),
        in_specs=[
            pl.BlockSpec((Bn, D), lambda g, j: (0, 0)),
            pl.BlockSpec((1, D, tn), lambda g, j: (g, 0, j)),
            pl.BlockSpec((1, 1, tn), lambda g, j: (g, 0, j)),
        ],
        out_specs=pl.BlockSpec((1, Bn, tn), lambda g, j: (g, 0, j)),
        out_shape=jax.ShapeDtypeStruct((G, Bn, N), F32),
        compiler_params=_cparams(("parallel", "parallel")),
        name="adaln",
    )(c, w, b.reshape(G, 1, N))


def _inproj_kernel(x_ref, mod_ref, wa_ref, wb_ref, wc_ref, ws_ref, wg_ref,
                   oa_ref, ob_ref, oc_ref, os_ref, og_ref):
    x = x_ref[0]
    shift = mod_ref[0, 0:1, :]
    scale = mod_ref[0, 1:2, :]
    h = (x * (1.0 + scale) + shift).astype(_MXU_DTYPE)
    for w_ref, o_ref in ((wa_ref, oa_ref), (wb_ref, ob_ref), (wc_ref, oc_ref),
                         (ws_ref, os_ref), (wg_ref, og_ref)):
        n = w_ref.shape[1]
        for j in range(0, n, 512):
            w = min(512, n - j)
            o_ref[0, :, j:j + w] = _dot(h, w_ref[:, j:j + w]).astype(o_ref.dtype)


def _inproj(x, mod, wa, wb, wc, ws, wg):
    Bn, S, D = x.shape
    tm = TM_PROJ
    ws_ = [wa, wb, wc, ws, wg]
    dts = [_MXU_DTYPE, _MXU_DTYPE, _MXU_DTYPE, F32, _MXU_DTYPE]
    return pl.pallas_call(
        _inproj_kernel,
        grid=(Bn, S // tm),
        in_specs=[pl.BlockSpec((1, tm, D), lambda b, i: (b, i, 0)),
                  pl.BlockSpec((1, 3, D), lambda b, i: (b, 0, 0))]
                 + [pl.BlockSpec(w.shape, lambda b, i: (0, 0)) for w in ws_],
        out_specs=[pl.BlockSpec((1, tm, w.shape[1]), lambda b, i: (b, i, 0)) for w in ws_],
        out_shape=[jax.ShapeDtypeStruct((Bn, S, w.shape[1]), dt) for w, dt in zip(ws_, dts)],
        compiler_params=_cparams(("parallel", "parallel")),
        name="inproj",
    )(x, mod, *ws_)


def _compress_kernel(r_ref, rs_ref, pa_ref, pb_ref, wa_ref, wb_ref, w2_ref, o_ref):
    r = r_ref[0].astype(F32)
    rs = rs_ref[0].astype(F32)
    outs = []
    for j in range(2):
        xa = (r + pa_ref[j]).astype(_MXU_DTYPE)
        xb = (rs + pb_ref[j]).astype(_MXU_DTYPE)
        hid = jax.nn.gelu(_dot(xa, wa_ref[j]) + _dot(xb, wb_ref[j]))
        outs.append(_dot(hid.astype(_MXU_DTYPE), w2_ref[j]))
    o_ref[0] = jnp.concatenate(outs, axis=1).astype(o_ref.dtype)


def _compress(r2, r2s, pa, pb, wa, wb, w2):
    Bn, NC, KW = r2.shape
    return pl.pallas_call(
        _compress_kernel,
        grid=(Bn,),
        in_specs=[pl.BlockSpec((1, NC, KW), lambda b: (b, 0, 0)),
                  pl.BlockSpec((1, NC, KW), lambda b: (b, 0, 0)),
                  pl.BlockSpec(pa.shape, lambda b: (0, 0, 0)),
                  pl.BlockSpec(pb.shape, lambda b: (0, 0, 0)),
                  pl.BlockSpec(wa.shape, lambda b: (0, 0, 0)),
                  pl.BlockSpec(wb.shape, lambda b: (0, 0, 0)),
                  pl.BlockSpec(w2.shape, lambda b: (0, 0, 0))],
        out_specs=pl.BlockSpec((1, NC, 2 * HEAD_DIM), lambda b: (b, 0, 0)),
        out_shape=jax.ShapeDtypeStruct((Bn, NC, 2 * HEAD_DIM), _MXU_DTYPE),
        compiler_params=_cparams(("parallel",)),
        name="nsa_compress",
    )(r2, r2s, pa, pb, wa, wb, w2)


def _online_t(s, m_ref, l_ref, acc_ref, v_t):
    m = m_ref[...]
    m_new = jnp.maximum(m, jnp.max(s, axis=0, keepdims=True))
    alpha = jnp.exp2(m - m_new)
    p = jnp.exp2(s - m_new)
    l_ref[...] = alpha * l_ref[...] + jnp.sum(p, axis=0, keepdims=True)
    acc_ref[...] = alpha * acc_ref[...] + _dot(v_t, p.astype(_MXU_DTYPE))
    m_ref[...] = m_new


def _reset(m_ref, l_ref, acc_ref):
    m_ref[...] = jnp.full(m_ref.shape, NEG, F32)
    l_ref[...] = jnp.zeros(l_ref.shape, F32)
    acc_ref[...] = jnp.zeros(acc_ref.shape, F32)


def _finish_t(m_ref, l_ref, acc_ref):
    return jnp.where(m_ref[...] > 0.5 * NEG, acc_ref[...] / jnp.maximum(l_ref[...], 1e-30), 0.0)


def _pair_loop(n, produce, consume, buf_a, buf_b):
    @pl.when(n > 0)
    def _():
        produce(0, buf_a)

    def pair(i, carry):
        k0 = 2 * i
        produce(k0 + 1, buf_b)
        consume(k0, buf_a)

        @pl.when(k0 + 1 < n)
        def _():
            produce(k0 + 2, buf_a)
            consume(k0 + 1, buf_b)
        return carry

    lax.fori_loop(0, (n + 1) // 2, pair, 0)


def _zero_extended_queries(q, n_heads, width):
    t = q.shape[0]
    pad = jnp.zeros((t, LANES - width), q.dtype)
    return jnp.concatenate(
        [jnp.concatenate([q[:, h * width:(h + 1) * width], pad], axis=1) for h in range(n_heads)],
        axis=0)


def _heads_to_rows_t(o_t, n_heads):
    t = o_t.shape[1] // n_heads
    stacked = jnp.concatenate([o_t[:, h * t:(h + 1) * t] for h in range(n_heads)], axis=0)
    return stacked.T


def _dsa_kernel(aq_ref, akv_ref, vt_ref, iq_ref, ik_ref, side_ref, bias_ref, cmask_ref, o_ref,
                keys_ref, hi_ref, lo_ref, acc_ref, m_ref, l_ref, run_ref, sa_ref, sb_ref, pa_ref,
                pb_ref, *, T, topk):
    qi = pl.program_id(1)
    nk = qi + 1
    H = A_HEADS

    q_z = _zero_extended_queries(aq_ref[0], H, HEAD_DIM)
    iq_z = _zero_extended_queries(iq_ref[0], IDX_HEADS, IDX_DIM)
    iw_t = side_ref[0].T[0:IDX_HEADS] * (IDX_HEADS ** -0.5 * IDX_DIM ** -0.5)

    def chunk_rows(kc):
        return pl.ds(pl.multiple_of(kc * T, T), T)

    def to_keys(kc, causal_mask):
        sc = jnp.maximum(_dot_nt(ik_ref[0, chunk_rows(kc), :], iq_z), 0.0)
        isc = iw_t[0:1] * sc[:, 0:T]
        for h in range(1, IDX_HEADS):
            isc = isc + iw_t[h:h + 1] * sc[:, h * T:(h + 1) * T]
        isc = isc + 0.0
        if causal_mask is not None:
            isc = isc + causal_mask
        bits = pltpu.bitcast(isc, jnp.int32)
        keys = jnp.where(bits < 0, bits ^ 0x7FFFFFFF, bits)
        keys_ref[kc] = keys
        hi_ref[kc] = lax.shift_right_arithmetic(keys, 16).astype(jnp.int16)

    def far_keys(kc, carry):
        to_keys(kc, None)
        return carry

    lax.fori_loop(0, qi, far_keys, 0)
    to_keys(qi, cmask_ref[...])

    n_acc = 4
    half_rows = 2 * SUBLANES

    def count16(plane_ref, cand, strict):
        cand16 = cand.astype(jnp.int16)
        one, zero = jnp.int16(1), jnp.int16(0)

        def body(kc, accs):
            k = plane_ref[kc]
            hit = jnp.where(k > cand16 if strict else k >= cand16, one, zero)
            parts = hit.reshape(T // half_rows, half_rows, T)
            accs = list(accs)
            for r in range(T // half_rows):
                accs[r % n_acc] = accs[r % n_acc] + parts[r]
            return tuple(accs)

        accs = lax.fori_loop(0, nk, body,
                             tuple(jnp.zeros((half_rows, T), jnp.int16) for _ in range(n_acc)))
        total = (accs[0] + accs[1]) + (accs[2] + accs[3])
        return jnp.sum(total.astype(F32), axis=0, keepdims=True)

    def search16(plane_ref, base_count):
        def bit_step(b, thr):
            cand = thr + lax.shift_left(jnp.int32(1), 15 - b)
            cnt = base_count + count16(plane_ref, cand, False)
            return jnp.where(cnt >= float(topk), cand, thr)
        return lax.fori_loop(0, 16, bit_step, jnp.full((1, T), -2 ** 15, jnp.int32))

    thr_hi = search16(hi_ref, 0.0)
    above_hi = count16(hi_ref, thr_hi, True)

    def low_plane(kc, carry):
        keys = keys_ref[kc]
        lo = (keys & 0xFFFF) - 2 ** 15
        same = lax.shift_right_arithmetic(keys, 16) == thr_hi
        lo_ref[kc] = jnp.where(same, lo, -2 ** 15).astype(jnp.int16)
        return carry

    lax.fori_loop(0, nk, low_plane, 0)
    thr_lo = search16(lo_ref, above_hi)
    thr = thr_hi * 2 ** 16 + (thr_lo + 2 ** 15)
    need = float(topk) - (above_hi + count16(lo_ref, thr_lo, True))

    row = lax.broadcasted_iota(jnp.int32, (T, T), 0)
    col = lax.broadcasted_iota(jnp.int32, (T, T), 1)
    lower = jnp.where(col < row, 1.0, 0.0).astype(_MXU_DTYPE)

    _reset(m_ref, l_ref, acc_ref)
    run_ref[...] = jnp.zeros(run_ref.shape, F32)

    def produce_attn(n):
        def produce(kc, bufs):
            s_buf, p_buf = bufs
            kcc = jnp.minimum(kc, n - 1)
            s_buf[...] = _dot_nt(akv_ref[0, chunk_rows(kcc), :], q_z)
            eqf = jnp.where(keys_ref[kcc] == thr, 1.0, 0.0).astype(_MXU_DTYPE)
            p_buf[...] = _dot(lower, eqf)
        return produce

    def attend(kc, bufs, bias):
        s_buf, p_buf = bufs
        keys = keys_ref[kc]
        eq = keys == thr
        run = run_ref[...]
        sel = ((keys > thr) | (eq & (p_buf[...] + run < need))) & (keys != NEG_INF_KEY)
        run_ref[...] = run + jnp.sum(jnp.where(eq, 1.0, 0.0), axis=0, keepdims=True)
        s = s_buf[...] if bias is None else s_buf[...] + bias
        s = jnp.where(jnp.concatenate([sel] * H, axis=1), s, NEG)
        _online_t(s, m_ref, l_ref, acc_ref, vt_ref[0, kc])

    n_far = jnp.maximum(qi - 1, 0)
    _pair_loop(n_far, produce_attn(n_far), lambda kc, bufs: attend(kc, bufs, None),
               (sa_ref, pa_ref), (sb_ref, pb_ref))

    @pl.when(qi >= 1)
    def _():
        produce_attn(nk)(qi - 1, (sa_ref, pa_ref))
    produce_attn(nk)(qi, (sb_ref, pb_ref))

    @pl.when(qi >= 1)
    def _():
        attend(qi - 1, (sa_ref, pa_ref), bias_ref[1])
    attend(qi, (sb_ref, pb_ref), bias_ref[0])

    o_ref[0] = _heads_to_rows_t(_finish_t(m_ref, l_ref, acc_ref), H).astype(o_ref.dtype)


def _dsa(apack, av_t, side, bias, cmask, T):
    Bn, S, _ = apack.shape
    topk = min(DSA_TOPK, S // 4)
    N = A_HEADS * T
    kern = functools.partial(_dsa_kernel, T=T, topk=topk)
    return pl.pallas_call(
        kern,
        grid=(Bn, S // T),
        in_specs=[pl.BlockSpec((1, T, 256), lambda b, i: (b, i, 0)),
                  pl.BlockSpec((1, S, 128), lambda b, i: (b, 0, 2)),
                  pl.BlockSpec((1,) + av_t.shape[1:], lambda b, i: (b, 0, 0, 0)),
                  pl.BlockSpec((1, T, 128), lambda b, i: (b, i, 3)),
                  pl.BlockSpec((1, S, 128), lambda b, i: (b, 0, 4)),
                  pl.BlockSpec((1, T, 128), lambda b, i: (b, i, 0)),
                  pl.BlockSpec(bias.shape, lambda b, i: (0, 0, 0)),
                  pl.BlockSpec(cmask.shape, lambda b, i: (0, 0))],
        out_specs=pl.BlockSpec((1, T, 256), lambda b, i: (b, i, 0)),
        out_shape=jax.ShapeDtypeStruct((Bn, S, 256), _MXU_DTYPE),
        scratch_shapes=[pltpu.VMEM((S // T, T, T), jnp.int32),
                        pltpu.VMEM((S // T, T, T), jnp.int16),
                        pltpu.VMEM((S // T, T, T), jnp.int16),
                        pltpu.VMEM((HEAD_DIM, N), F32),
                        pltpu.VMEM((1, N), F32), pltpu.VMEM((1, N), F32), pltpu.VMEM((1, T), F32),
                        pltpu.VMEM((T, N), F32), pltpu.VMEM((T, N), F32),
                        pltpu.VMEM((T, T), F32), pltpu.VMEM((T, T), F32)],
        compiler_params=_cparams(("parallel", "arbitrary")),
        name="dsa_attention",
    )(apack, apack, av_t, apack, apack, side, bias, cmask)


def _nsa_kernel(bq_ref, kvc_ref, vct_ref, kvs_ref, vst_ref, kvw_ref, vwt_ref, side_ref,
                bias_ref, wbias_ref, ovt_ref, o_ref, acc_ref, m_ref, l_ref, sa_ref, sb_ref,
                pa_ref, pb_ref, *, T, n_cmp, n_sel, n_win):
    qi = pl.program_id(1)
    nk = qi + 1
    H = B_HEADS
    NC = kvc_ref.shape[1]
    NS = ovt_ref.shape[0]
    log2_blk = int(math.log2(SLC_BLOCK))

    q_z = _zero_extended_queries(bq_ref[0], H, HEAD_DIM)
    t_q = qi * T + lax.broadcasted_iota(jnp.int32, (1, T), 1)
    t_st = jnp.concatenate([t_q] * H, axis=1)

    def chunk_rows(kc):
        return pl.ds(pl.multiple_of(kc * T, T), T)

    n_idx = lax.broadcasted_iota(jnp.int32, (NC, 1), 0)
    cvalid = (n_idx * CMP_STRIDE + (CMP_BLOCK - 1) <= t_st) & (n_idx < n_cmp)
    lc = jnp.where(cvalid, _dot_nt(kvc_ref[0], q_z), NEG)
    mc = jnp.max(lc, axis=0, keepdims=True)
    pc = jnp.where(cvalid, jnp.exp2(lc - mc), 0.0)
    pc = pc / jnp.maximum(jnp.sum(pc, axis=0, keepdims=True), 1e-30)
    o_cmp = _dot(vct_ref[0], pc.astype(_MXU_DTYPE))

    psum = pc[:, 0:T]
    for h in range(1, H):
        psum = psum + pc[:, h * T:(h + 1) * T]
    p_hi = psum.astype(_MXU_DTYPE)
    p_lo = (psum - p_hi.astype(F32)).astype(_MXU_DTYPE)
    imp = _dot(ovt_ref[...], p_hi) + _dot(ovt_ref[...], p_lo)

    blk = lax.broadcasted_iota(jnp.int32, (NS, T), 0)
    blkf = blk.astype(F32)
    cur = lax.shift_right_logical(t_q, log2_blk)
    forced = (blk == 0) | (blk == cur) | (blk == cur - 1)
    val = jnp.where(blk <= cur, jnp.where(forced, FORCE_SCORE, imp), -jnp.inf)

    def pick_one(_, carry):
        val, sel = carry
        mx = jnp.max(val, axis=0, keepdims=True)
        first = jnp.min(jnp.where(val == mx, blkf, float(NS)), axis=0, keepdims=True)
        pick = blkf == first
        sel = jnp.where(pick & (mx > -jnp.inf), 1.0, sel)
        return jnp.where(pick, -jnp.inf, val), sel

    _, sel = lax.fori_loop(0, n_sel, pick_one, (val, jnp.zeros((NS, T), F32)))
    sel = sel.astype(_MXU_DTYPE)

    e_row = lax.shift_right_logical(lax.broadcasted_iota(jnp.int32, (T, NS), 0), log2_blk)
    e_col = lax.broadcasted_iota(jnp.int32, (T, NS), 1)
    bpc = T // SLC_BLOCK

    _reset(m_ref, l_ref, acc_ref)

    def produce_slc(n):
        def produce(kc, bufs):
            s_buf, p_buf = bufs
            kcc = jnp.minimum(kc, n - 1)
            s_buf[...] = _dot_nt(kvs_ref[0, chunk_rows(kcc), :], q_z)
            expand = jnp.where(e_col == kcc * bpc + e_row, 1.0, 0.0).astype(_MXU_DTYPE)
            p_buf[...] = _dot(expand, sel)
        return produce

    def attend_slc(kc, bufs, bias):
        s_buf, p_buf = bufs
        addm = jnp.where(p_buf[...] > 0.5, 0.0, NEG)
        s = s_buf[...] + jnp.concatenate([addm] * H, axis=1)
        if bias is not None:
            s = s + bias
        _online_t(s, m_ref, l_ref, acc_ref, vst_ref[0, kc])

    n_far = jnp.maximum(qi - 1, 0)
    _pair_loop(n_far, produce_slc(n_far), lambda kc, bufs: attend_slc(kc, bufs, None),
               (sa_ref, pa_ref), (sb_ref, pb_ref))

    @pl.when(qi >= 1)
    def _():
        produce_slc(nk)(qi - 1, (sa_ref, pa_ref))
    produce_slc(nk)(qi, (sb_ref, pb_ref))

    @pl.when(qi >= 1)
    def _():
        attend_slc(qi - 1, (sa_ref, pa_ref), bias_ref[1])
    attend_slc(qi, (sb_ref, pb_ref), bias_ref[0])
    o_slc = _finish_t(m_ref, l_ref, acc_ref)

    _reset(m_ref, l_ref, acc_ref)

    def produce_win(delta, buf):
        buf[...] = _dot_nt(kvw_ref[0, chunk_rows(jnp.maximum(qi - delta, 0)), :], q_z)

    def attend_win(delta, buf):
        kc = qi - delta
        s = buf[...] + wbias_ref[delta] + jnp.where(kc >= 0, 0.0, NEG)
        _online_t(s, m_ref, l_ref, acc_ref, vwt_ref[0, jnp.maximum(kc, 0)])

    bufs = (sa_ref, sb_ref)
    produce_win(0, bufs[0])
    for delta in range(n_win):
        if delta + 1 < n_win:
            produce_win(delta + 1, bufs[(delta + 1) % 2])
        attend_win(delta, bufs[delta % 2])
    o_win = _finish_t(m_ref, l_ref, acc_ref)

    g = jax.nn.sigmoid(side_ref[0].T[IDX_HEADS:IDX_HEADS + 3 * H])
    outs = []
    for h in range(H):
        sl = slice(h * T, (h + 1) * T)
        outs.append(g[3 * h:3 * h + 1] * o_cmp[:, sl] + g[3 * h + 1:3 * h + 2] * o_slc[:, sl]
                    + g[3 * h + 2:3 * h + 3] * o_win[:, sl])
    o_ref[0] = jnp.concatenate(outs, axis=0).T.astype(o_ref.dtype)


def _nsa(bpack, kvc, vc_t, vs_t, vw_t, side, bias, wbias, overlap_t, T, n_cmp, n_sel):
    Bn, S, _ = bpack.shape
    NC = kvc.shape[1]
    N = B_HEADS * T
    kern = functools.partial(_nsa_kernel, T=T, n_cmp=n_cmp, n_sel=n_sel, n_win=wbias.shape[0])
    vtspec = lambda a: pl.BlockSpec((1,) + a.shape[1:], lambda b, i: (b, 0, 0, 0))
    return pl.pallas_call(
        kern,
        grid=(Bn, S // T),
        in_specs=[pl.BlockSpec((1, T, 256), lambda b, i: (b, i, 0)),
                  pl.BlockSpec((1, NC, 128), lambda b, i: (b, 0, 0)),
                  pl.BlockSpec((1, HEAD_DIM, NC), lambda b, i: (b, 0, 0)),
                  pl.BlockSpec((1, S, 128), lambda b, i: (b, 0, 3)),
                  vtspec(vs_t),
                  pl.BlockSpec((1, S, 128), lambda b, i: (b, 0, 4)),
                  vtspec(vw_t),
                  pl.BlockSpec((1, T, 128), lambda b, i: (b, i, 0)),
                  pl.BlockSpec(bias.shape, lambda b, i: (0, 0, 0)),
                  pl.BlockSpec(wbias.shape, lambda b, i: (0, 0, 0)),
                  pl.BlockSpec(overlap_t.shape, lambda b, i: (0, 0))],
        out_specs=pl.BlockSpec((1, T, 256), lambda b, i: (b, i, 0)),
        out_shape=jax.ShapeDtypeStruct((Bn, S, 256), _MXU_DTYPE),
        scratch_shapes=[pltpu.VMEM((HEAD_DIM, N), F32),
                        pltpu.VMEM((1, N), F32), pltpu.VMEM((1, N), F32),
                        pltpu.VMEM((T, N), F32), pltpu.VMEM((T, N), F32),
                        pltpu.VMEM((T, T), F32), pltpu.VMEM((T, T), F32)],
        compiler_params=_cparams(("parallel", "arbitrary")),
        name="nsa_attention",
    )(bpack, kvc, vc_t, bpack, vs_t, bpack, vw_t, side, bias, wbias, overlap_t)


def _diff_kernel(cq_ref, ck_ref, vt_ref, bias_ref, dl_ref, g_ref, o_ref, acc_ref, m_ref, l_ref,
                 sa_ref, sb_ref, *, T, lam_init):
    qi = pl.program_id(1)
    nk = qi + 1
    H = C_HEADS
    dl = dl_ref[...]
    lam = (jnp.exp(jnp.sum(dl[0:1] * dl[1:2], axis=1, keepdims=True))
           - jnp.exp(jnp.sum(dl[2:3] * dl[3:4], axis=1, keepdims=True)) + lam_init)

    cq = cq_ref[0]
    lane = lax.broadcasted_iota(jnp.int32, (T, 2 * HEAD_DIM), 1)
    q_z = []
    for h in range(H):
        qh = cq[:, h * 2 * HEAD_DIM:(h + 1) * 2 * HEAD_DIM]
        zero = jnp.zeros_like(qh)
        q_z.append(jnp.concatenate([jnp.where(lane < HEAD_DIM, qh, zero),
                                    jnp.where(lane >= HEAD_DIM, qh, zero)], axis=0))

    for h in range(H):
        _reset(m_ref.at[h], l_ref.at[h], acc_ref.at[h])

    def produce(n):
        def fn(kc, dst):
            start = pl.multiple_of(jnp.minimum(kc, n - 1) * T, T)
            kch = ck_ref[0, pl.ds(start, T), :]
            for h in range(H):
                dst[h] = _dot_nt(kch[:, h * 2 * HEAD_DIM:(h + 1) * 2 * HEAD_DIM], q_z[h])
        return fn

    def consume(kc, src, delta):
        v_t = vt_ref[0, kc]
        for h in range(H):
            s = src[h]
            if delta is not None:
                b = bias_ref[delta, h]
                s = jnp.concatenate([s[:, 0:T] + b, s[:, T:2 * T] + b], axis=1)
            _online_t(s, m_ref.at[h], l_ref.at[h], acc_ref.at[h],
                      v_t[h * C_VDIM:(h + 1) * C_VDIM])

    n_far = jnp.maximum(qi - 1, 0)
    _pair_loop(n_far, produce(n_far), lambda kc, src: consume(kc, src, None), sa_ref, sb_ref)

    @pl.when(qi >= 1)
    def _():
        produce(nk)(qi - 1, sa_ref)
    produce(nk)(qi, sb_ref)

    @pl.when(qi >= 1)
    def _():
        consume(qi - 1, sa_ref, 1)
    consume(qi, sb_ref, 0)

    outs = []
    for h in range(H):
        o = _finish_t(m_ref.at[h], l_ref.at[h], acc_ref.at[h])
        oc = o[:, 0:T] - lam * o[:, T:2 * T]
        rms = lax.rsqrt(jnp.mean(jnp.square(oc), axis=0, keepdims=True) + LN_EPS)
        outs.append(oc * rms * g_ref[...] * (1.0 - lam_init))
    o_ref[0] = jnp.concatenate(outs, axis=0).T.astype(o_ref.dtype)


def _diff(cpack, cv_t, bias, dl, g, T, lam_init):
    Bn, S, _ = cpack.shape
    W = C_HEADS * C_VDIM
    kern = functools.partial(_diff_kernel, T=T, lam_init=lam_init)
    return pl.pallas_call(
        kern,
        grid=(Bn, S // T),
        in_specs=[pl.BlockSpec((1, T, W), lambda b, i: (b, i, 0)),
                  pl.BlockSpec((1, S, W), lambda b, i: (b, 0, 1)),
                  pl.BlockSpec((1,) + cv_t.shape[1:], lambda b, i: (b, 0, 0, 0)),
                  pl.BlockSpec(bias.shape, lambda b, i: (0, 0, 0, 0)),
                  pl.BlockSpec(dl.shape, lambda b, i: (0, 0)),
                  pl.BlockSpec(g.shape, lambda b, i: (0, 0))],
        out_specs=pl.BlockSpec((1, T, W), lambda b, i: (b, i, 0)),
        out_shape=jax.ShapeDtypeStruct((Bn, S, W), _MXU_DTYPE),
        scratch_shapes=[pltpu.VMEM((C_HEADS, C_VDIM, 2 * T), F32),
                        pltpu.VMEM((C_HEADS, 1, 2 * T), F32),
                        pltpu.VMEM((C_HEADS, 1, 2 * T), F32),
                        pltpu.VMEM((C_HEADS, T, 2 * T), F32),
                        pltpu.VMEM((C_HEADS, T, 2 * T), F32)],
        compiler_params=_cparams(("parallel", "arbitrary")),
        name="diff_attention",
    )(cpack, cpack, cv_t, bias, dl, g)


def _layer_norm(v, g, b):
    mu = jnp.mean(v, axis=1, keepdims=True)
    d = v - mu
    var = jnp.mean(jnp.square(d), axis=1, keepdims=True)
    return d * lax.rsqrt(var + LN_EPS) * g + b


def _split_hi_lo(v):
    hi = v.astype(_MXU_DTYPE)
    lo = (v - hi.astype(F32)).astype(_MXU_DTYPE)
    return hi, lo


def _outproj_kernel(oa_ref, ob_ref, oc_ref, mg_ref, x_ref, mod1_ref, mod2_ref,
                    wa_ref, wb_ref, wc_ref, wo_ref, ln_ref, rw_ref, rb_ref,
                    x1_ref, h2_ref, tr_ref, tw_ref, cnt_ref, run_ref, *, dn_alpha, n_experts):
    D = x_ref.shape[2]
    tm = x_ref.shape[1]
    first_step = (pl.program_id(0) == 0) & (pl.program_id(1) == 0)

    @pl.when(first_step)
    def _():
        run_ref[...] = jnp.zeros(run_ref.shape, F32)

    ya = _dot(oa_ref[0], wa_ref[...])
    yb = _dot(ob_ref[0], wb_ref[...])
    yc = _dot(oc_ref[0], wc_ref[...])
    mg = mg_ref[0].astype(F32)
    merged = (jax.nn.sigmoid(mg[:, 0:D]) * ya + jax.nn.sigmoid(mg[:, D:2 * D]) * yb
              + jax.nn.sigmoid(mg[:, 2 * D:3 * D]) * yc)
    y = _dot(merged.astype(_MXU_DTYPE), wo_ref[...])
    gate1 = mod1_ref[0, 2:3, :]
    x1 = _layer_norm(dn_alpha * x_ref[0] + (1.0 + gate1) * y, ln_ref[0:1, :], ln_ref[1:2, :])
    x1_ref[0] = x1
    h2 = x1 * (1.0 + mod2_ref[0, 1:2, :]) + mod2_ref[0, 0:1, :]
    h2_ref[0] = h2.astype(h2_ref.dtype)

    h_hi, h_lo = _split_hi_lo(h2)
    w_hi, w_lo = _split_hi_lo(rw_ref[...])
    logits = _dot(h_hi, w_hi) + _dot(h_hi, w_lo) + _dot(h_lo, w_hi) + rb_ref[...]
    lane = lax.broadcasted_iota(jnp.int32, logits.shape, 1)
    lanef = lane.astype(F32)
    work = jnp.where(lane < n_experts, logits, -jnp.inf)
    tv = jnp.zeros(logits.shape, F32)
    firsts, onehots = [], []
    v0 = None
    for k in range(TOP_K):
        mx = jnp.max(work, axis=1, keepdims=True)
        first = jnp.min(jnp.where(work == mx, lanef, float(LANES)), axis=1, keepdims=True)
        if k == 0:
            v0 = mx
        tv = jnp.where(lane == k, jnp.exp(mx - v0), tv)
        hit = lanef == first
        work = jnp.where(hit, -jnp.inf, work)
        firsts.append(first)
        onehots.append(jnp.where(hit, 1.0, 0.0))
    tw_ref[0] = tv / jnp.sum(tv, axis=1, keepdims=True)

    chosen = onehots[0] + onehots[1] + onehots[2] + onehots[3]
    r_i = lax.broadcasted_iota(jnp.int32, (tm, tm), 0)
    c_i = lax.broadcasted_iota(jnp.int32, (tm, tm), 1)
    earlier = jnp.where(c_i < r_i, 1.0, 0.0).astype(_MXU_DTYPE)
    base = _dot(earlier, chosen.astype(_MXU_DTYPE)) + run_ref[0:1, :]
    tr = jnp.zeros(logits.shape, F32)
    for k in range(TOP_K):
        rank = jnp.sum(onehots[k] * base, axis=1, keepdims=True)
        tr = jnp.where(lane == k, firsts[k], tr)
        tr = jnp.where(lane == TOP_K + k, rank, tr)
    tr_ref[0] = tr.astype(jnp.int32)
    run_ref[...] = run_ref[...] + jnp.sum(chosen, axis=0, keepdims=True)
    cnt_ref[...] = run_ref[...]


def _outproj(oa, ob, oc, mg, x, mod1, mod2, wa, wb, wc, wo, ln, rw, rb, dn_alpha, n_experts):
    Bn, S, D = x.shape
    tm = TM_PROJ
    row = lambda w: pl.BlockSpec((1, tm, w), lambda b, i: (b, i, 0))
    full = lambda a: pl.BlockSpec(a.shape, lambda b, i: (0,) * a.ndim)
    modspec = pl.BlockSpec((1, 3, D), lambda b, i: (b, 0, 0))
    kern = functools.partial(_outproj_kernel, dn_alpha=dn_alpha, n_experts=n_experts)
    return pl.pallas_call(
        kern,
        grid=(Bn, S // tm),
        in_specs=[row(oa.shape[2]), row(ob.shape[2]), row(oc.shape[2]), row(mg.shape[2]), row(D),
                  modspec, modspec, full(wa), full(wb), full(wc), full(wo), full(ln), full(rw),
                  full(rb)],
        out_specs=[row(D), row(D), row(LANES), row(LANES),
                   pl.BlockSpec((SUBLANES, LANES), lambda b, i: (0, 0))],
        out_shape=[jax.ShapeDtypeStruct((Bn, S, D), F32),
                   jax.ShapeDtypeStruct((Bn, S, D), _MXU_DTYPE),
                   jax.ShapeDtypeStruct((Bn, S, LANES), jnp.int32),
                   jax.ShapeDtypeStruct((Bn, S, LANES), F32),
                   jax.ShapeDtypeStruct((SUBLANES, LANES), F32)],
        scratch_shapes=[pltpu.VMEM((SUBLANES, LANES), F32)],
        compiler_params=_cparams(("arbitrary", "arbitrary")),
        name="merge_outproj_ln_router",
    )(oa, ob, oc, mg, x, mod1, mod2, wa, wb, wc, wo, ln, rw, rb)


def _deinterleave_kernel(w_ref, p_ref, og_ref, ou_ref):
    n = w_ref.shape[1]
    for j in range(n // (2 * LANES)):
        blk = w_ref[:, j * 2 * LANES:(j + 1) * 2 * LANES].astype(_MXU_DTYPE)
        r = _dot(blk, p_ref[...])
        og_ref[:, j * LANES:(j + 1) * LANES] = r[:, 0:LANES].astype(og_ref.dtype)
        ou_ref[:, j * LANES:(j + 1) * LANES] = r[:, LANES:2 * LANES].astype(ou_ref.dtype)


def _deinterleave(w_all, part, n_parts):
    N2 = w_all.shape[1]
    R = w_all.shape[0] // n_parts
    tr = 512
    first = part * (R // tr)
    perm = np.zeros((2 * LANES, 2 * LANES), np.float32)
    perm[2 * np.arange(LANES), np.arange(LANES)] = 1.0
    perm[2 * np.arange(LANES) + 1, LANES + np.arange(LANES)] = 1.0
    return pl.pallas_call(
        _deinterleave_kernel,
        grid=(R // tr,),
        in_specs=[pl.BlockSpec((tr, N2), lambda i: (first + i, 0)),
                  pl.BlockSpec(perm.shape, lambda i: (0, 0))],
        out_specs=[pl.BlockSpec((tr, N2 // 2), lambda i: (i, 0))] * 2,
        out_shape=[jax.ShapeDtypeStruct((R, N2 // 2), _MXU_DTYPE)] * 2,
        compiler_params=_cparams(("parallel",)),
        name="expert_weight_deinterleave",
    )(w_all, jnp.asarray(perm).astype(_MXU_DTYPE))


def _moe_kernel(te_ref, tv_ref, x_ref, wg_ref, wu_ref, wd_ref, bg_ref, bu_ref, bd_ref, o_ref):
    i = pl.program_id(0)

    @pl.when(tv_ref[i] > 0)
    def _():
        x = x_ref[...]
        dff = wg_ref.shape[2]
        ck = 512
        y = jnp.zeros(o_ref.shape, F32)
        for j in range(0, dff, ck):
            gate = jnp.minimum(_dot(x, wg_ref[0, :, j:j + ck]) + bg_ref[0, :, j:j + ck], SWIGLU_LIMIT)
            up = jnp.clip(_dot(x, wu_ref[0, :, j:j + ck]) + bu_ref[0, :, j:j + ck],
                          -SWIGLU_LIMIT, SWIGLU_LIMIT)
            act = (up + 1.0) * (gate * jax.nn.sigmoid(SWIGLU_ALPHA * gate))
            y = y + _dot(act.astype(_MXU_DTYPE), wd_ref[0, j:j + ck, :])
        o_ref[...] = (y + bd_ref[0]).astype(o_ref.dtype)


def _moe(xs, tile_expert, tile_valid, wg, wu, wd_all, wd_first, bg, bu, bd):
    R, D = xs.shape
    tm = TM_MOE
    nt = R // tm
    wspec = lambda a: pl.BlockSpec((1,) + a.shape[1:], lambda i, te, tv: (te[i], 0, 0))
    wdspec = pl.BlockSpec((1,) + wd_all.shape[1:], lambda i, te, tv: (wd_first + te[i], 0, 0))
    grid_spec = pltpu.PrefetchScalarGridSpec(
        num_scalar_prefetch=2,
        grid=(nt,),
        in_specs=[pl.BlockSpec((tm, D), lambda i, te, tv: (i, 0)),
                  wspec(wg), wspec(wu), wdspec, wspec(bg), wspec(bu), wspec(bd)],
        out_specs=pl.BlockSpec((tm, D), lambda i, te, tv: (i, 0)),
    )
    return pl.pallas_call(
        _moe_kernel,
        grid_spec=grid_spec,
        out_shape=jax.ShapeDtypeStruct((R, D), _MXU_DTYPE),
        compiler_params=_cparams(("arbitrary",)),
        name="moe_grouped_mlp",
    )(tile_expert, tile_valid, xs, wg, wu, wd_all, bg, bu, bd)


def _combine_kernel(yg_ref, tw_ref, x_ref, mod_ref, ln_ref, o_ref, *, dn_alpha):
    tw = tw_ref[0]
    y = tw[:, 0:1] * yg_ref[0, 0].astype(F32)
    for k in range(1, TOP_K):
        y = y + tw[:, k:k + 1] * yg_ref[k, 0].astype(F32)
    gate = mod_ref[0, 2:3, :]
    o_ref[0] = _layer_norm(dn_alpha * x_ref[0] + (1.0 + gate) * y, ln_ref[0:1, :], ln_ref[1:2, :])


def _combine(yg, tw, x, mod, ln, dn_alpha):
    Bn, S, D = x.shape
    tm = TM_PROJ
    row = lambda w: pl.BlockSpec((1, tm, w), lambda b, i: (b, i, 0))
    return pl.pallas_call(
        functools.partial(_combine_kernel, dn_alpha=dn_alpha),
        grid=(Bn, S // tm),
        in_specs=[pl.BlockSpec((TOP_K, 1, tm, D), lambda b, i: (0, b, i, 0)), row(LANES), row(D),
                  pl.BlockSpec((1, 3, D), lambda b, i: (b, 0, 0)),
                  pl.BlockSpec(ln.shape, lambda b, i: (0, 0))],
        out_specs=row(D),
        out_shape=jax.ShapeDtypeStruct((Bn, S, D), F32),
        compiler_params=_cparams(("parallel", "parallel")),
        name="moe_combine_ln",
    )(yg, tw, x, mod, ln)


def _bucket_np(dist):
    exact = N_BUCKETS // 2
    n = np.maximum(dist, 0)
    nf = np.maximum(n, 1).astype(np.float32)
    large = exact + (np.log(nf / exact) / math.log(MAX_DISTANCE / exact)
                     * (N_BUCKETS - exact)).astype(np.int32)
    large = np.minimum(large, N_BUCKETS - 1)
    return np.where(n < exact, n, large)


def _bias_tiles_t(tab, T, n_delta, window=None, shift_far=True):
    H = tab.shape[1]
    L = 2 * T
    k = np.arange(L)
    diff = np.where(k < T, k, k - L)
    d = diff[None, :] + (np.arange(n_delta) * T)[:, None]
    ok = d >= 0 if window is None else (d >= 0) & (d < window)
    onehot = np.eye(N_BUCKETS, dtype=np.float32)[_bucket_np(d)]
    prof = jnp.einsum("dlb,bh->dhl", jnp.asarray(onehot), tab, precision=lax.Precision.HIGHEST)
    if shift_far:
        prof = prof - tab[N_BUCKETS - 1][None, :, None]
    prof = jnp.where(jnp.asarray(ok)[:, None, :], prof * LOG2E, NEG)
    rolled = jnp.tile(prof, (1, 1, T))[:, :, :T * (L - 1)].reshape(n_delta, H, T, L - 1)
    return rolled[:, :, :, :T]


def _stack_tiles(tiles):
    n, H, T, _ = tiles.shape
    return jnp.transpose(tiles, (0, 2, 1, 3)).reshape(n, T, H * T)


def _slc_overlap_np(n_cmp, n_slc, rows):
    start = np.arange(n_cmp) * CMP_STRIDE
    end = start + CMP_BLOCK
    bs = np.arange(n_slc) * SLC_BLOCK
    ov = (start[:, None] < bs[None, :] + SLC_BLOCK) & (end[:, None] > bs[None, :])
    out = np.zeros((rows, n_slc), np.float32)
    out[:n_cmp] = ov
    return out


def _chunked_t(v, T):
    Bn, S, d = v.shape
    return jnp.transpose(v.reshape(Bn, S // T, T, d), (0, 1, 3, 2))


def _routing(top_i, rank, counts, tm):
    N = top_i.shape[0]
    E = counts.shape[0]
    A = N * TOP_K
    nt = A // tm + E
    padded = ((counts + tm - 1) // tm) * tm
    starts = jnp.cumsum(counts) - counts
    pstarts = jnp.cumsum(padded) - padded
    pend = pstarts + padded
    tile_start = jnp.arange(nt, dtype=jnp.int32) * tm
    tile_valid = (tile_start < pend[-1]).astype(jnp.int32)
    te = jnp.minimum(jnp.sum((tile_start[:, None] >= pend[None, :]).astype(jnp.int32), axis=1), E - 1)
    last_valid = jnp.max(jnp.where(tile_valid > 0, te, 0))
    tile_expert = jnp.where(tile_valid > 0, te, last_valid).astype(jnp.int32)
    onehot_t = tile_expert[:, None] == jnp.arange(E)[None, :]
    pick = lambda tab: jnp.sum(jnp.where(onehot_t, tab[None, :], 0), axis=1)
    order = jnp.argsort(top_i.reshape(A), stable=True).astype(jnp.int32)
    in_group = (tile_start - pick(pstarts))[:, None] + jnp.arange(tm, dtype=jnp.int32)[None, :]
    ok = (in_group < pick(counts)[:, None]) & (tile_valid[:, None] > 0)
    slot = jnp.clip(pick(starts)[:, None] + in_group, 0, A - 1)
    row_tok = jnp.where(ok, jnp.take(order, slot.reshape(-1), mode="clip").reshape(nt, tm) // TOP_K, 0)
    onehot_a = top_i[:, :, None] == jnp.arange(E)[None, None, :]
    pos = rank + jnp.sum(jnp.where(onehot_a, pstarts[None, None, :], 0), axis=2)
    return tile_expert, tile_valid, row_tok.reshape(nt * tm), pos.astype(jnp.int32)


def kernel(x, c, rel_bias, mod_attn_w, mod_attn_b, w_in, cmp_pos, cmp_w1, cmp_w2, diff_lambda,
           diff_norm_g, w_branch_a, w_branch_b, w_branch_c, w_out, ln1_g, ln1_b, mod_ffn_w,
           mod_ffn_b, router_w, router_b, exp_w_gu, exp_b_gu, exp_w_down, exp_b_down, ln2_g, ln2_b):
    Bn, S, D = x.shape
    L = w_in.shape[0]
    E = exp_w_gu.shape[1]
    T = T_ATT
    cdt = _MXU_DTYPE
    dn_alpha = (2 * L) ** 0.25
    assert S % T == 0 and S % TM_PROJ == 0 and (Bn * S * TOP_K) % TM_MOE == 0
    assert (S // SLC_BLOCK) % SUBLANES == 0 and E <= LANES and T >= MAX_DISTANCE

    mod_w = jnp.stack([mod_attn_w, mod_ffn_w], axis=1).reshape(2 * L, D, 3 * D)
    mod_b = jnp.stack([mod_attn_b, mod_ffn_b], axis=1).reshape(2 * L, 3 * D)
    mods = _adaln(c, mod_w, mod_b).reshape(2 * L, Bn, 3, D)

    n_win = WINDOW // T + 1
    tab_b = rel_bias[:, A_HEADS:A_HEADS + B_HEADS]
    bias_a = _stack_tiles(_bias_tiles_t(rel_bias[:, :A_HEADS], T, 2))
    bias_b = _stack_tiles(_bias_tiles_t(tab_b, T, 2))
    bias_w = _stack_tiles(_bias_tiles_t(tab_b, T, n_win, WINDOW, shift_far=False))
    bias_c = _bias_tiles_t(rel_bias[:, A_HEADS + B_HEADS:], T, 2)
    jj = np.arange(T)
    cmask = jnp.asarray(np.where(jj[None, :] >= jj[:, None], 0.0, -np.inf).astype(np.float32))

    n_cmp = (S - CMP_BLOCK) // CMP_STRIDE + 1
    n_slc = S // SLC_BLOCK
    n_sel = min(SLC_TOPN, n_slc)
    NC = S // CMP_STRIDE
    overlap_t = jnp.asarray(_slc_overlap_np(n_cmp, n_slc, NC).T).astype(cdt)

    rows16 = CMP_STRIDE * 2 * HEAD_DIM
    w_gu_all = exp_w_gu.reshape(L * E * D, -1)
    w_down_all = exp_w_down.astype(cdt).reshape(L * E, exp_w_down.shape[2], D)

    for l in range(L):
        lam_init = 0.8 - 0.6 * math.exp(-0.3 * l)
        wl = w_in[l]
        wa = jnp.pad(jnp.concatenate([wl[:, 0:256] * Q_FOLD, wl[:, 256:544]], axis=1),
                     ((0, 0), (0, 96))).astype(cdt)
        wb = jnp.concatenate([wl[:, 548:804] * Q_FOLD, wl[:, 804:1188]], axis=1).astype(cdt)
        wc = jnp.concatenate([wl[:, 1200:1712] * Q_FOLD, wl[:, 1712:2736]], axis=1).astype(cdt)
        ws = jnp.pad(jnp.concatenate([wl[:, 544:548], wl[:, 1188:1200]], axis=1),
                     ((0, 0), (0, LANES - 16))).astype(cdt)
        wg = wl[:, 2736:].astype(cdt)

        apack, bpack, cpack, side, mg = _inproj(x, mods[2 * l], wa, wb, wc, ws, wg)

        r2 = bpack[:, :, 256:384].reshape(Bn, NC, rows16)
        r2s = jnp.concatenate([r2[:, 1:], jnp.zeros_like(r2[:, :1])], axis=1)
        half = CMP_BLOCK // 2
        zpad = jnp.zeros((half, HEAD_DIM), F32)
        pa, pb, wca, wcb = [], [], [], []
        for j in range(2):
            pos = cmp_pos[l, j]
            w1 = cmp_w1[l, j].reshape(CMP_BLOCK, HEAD_DIM, CMP_HIDDEN)
            zw = jnp.zeros((half, HEAD_DIM, CMP_HIDDEN), F32)
            kv = (lambda a, z: jnp.concatenate([a, z], axis=1)) if j == 0 else \
                 (lambda a, z: jnp.concatenate([z, a], axis=1))
            pa.append(kv(pos[:half], zpad).reshape(1, rows16))
            pb.append(kv(pos[half:], zpad).reshape(1, rows16))
            wca.append(kv(w1[:half], zw).reshape(rows16, CMP_HIDDEN))
            wcb.append(kv(w1[half:], zw).reshape(rows16, CMP_HIDDEN))
        kvc = _compress(r2, r2s, jnp.stack(pa), jnp.stack(pb), jnp.stack(wca).astype(cdt),
                        jnp.stack(wcb).astype(cdt), cmp_w2[l].astype(cdt))

        oa = _dsa(apack, _chunked_t(apack[:, :, 320:384], T), side, bias_a, cmask, T)
        ob = _nsa(bpack, kvc, jnp.transpose(kvc[:, :, HEAD_DIM:], (0, 2, 1)),
                  _chunked_t(bpack[:, :, 448:512], T), _chunked_t(bpack[:, :, 576:640], T),
                  side, bias_b, bias_w, overlap_t, T, n_cmp, n_sel)
        oc = _diff(cpack, _chunked_t(cpack[:, :, 1024:1536], T), bias_c, diff_lambda[l],
                   diff_norm_g[l].reshape(C_VDIM, 1), T, lam_init)

        rw = jnp.pad(router_w[l], ((0, 0), (0, LANES - E)))
        rb = jnp.pad(router_b[l], (0, LANES - E)).reshape(1, LANES)
        x1, h2, tr, tw, cnt = _outproj(
            oa, ob, oc, mg, x, mods[2 * l], mods[2 * l + 1],
            w_branch_a[l].astype(cdt), w_branch_b[l].astype(cdt), w_branch_c[l].astype(cdt),
            w_out[l].astype(cdt), jnp.stack([ln1_g[l], ln1_b[l]]), rw, rb, dn_alpha, E)

        N = Bn * S
        tr = tr.reshape(N, LANES)
        tile_expert, tile_valid, row_tok, pos = _routing(
            tr[:, 0:TOP_K], tr[:, TOP_K:2 * TOP_K], cnt[0, :E].astype(jnp.int32), TM_MOE)
        xs = jnp.take(h2.reshape(N, D), row_tok, axis=0, mode="clip")
        wgate, wup = _deinterleave(w_gu_all, l, L)
        bgu = exp_b_gu[l].reshape(E, 1, -1, 2)
        ys = _moe(xs, tile_expert, tile_valid,
                  wgate.reshape(E, D, -1), wup.reshape(E, D, -1), w_down_all, l * E,
                  bgu[..., 0], bgu[..., 1], exp_b_down[l].reshape(E, 1, D))
        yg = jnp.take(ys, pos.T.reshape(-1), axis=0, mode="clip").reshape(TOP_K, Bn, S, D)
        x = _combine(yg, tw, x1, mods[2 * l + 1], jnp.stack([ln2_g[l], ln2_b[l]]), dn_alpha)
    return x
```

```python
import functools
import math

import numpy as np
import jax
import jax.numpy as jnp
from jax import lax
from jax.experimental import pallas as pl
from jax.experimental.pallas import tpu as pltpu

F32 = jnp.float32
_MXU_DTYPE = jnp.bfloat16

HEAD_DIM = 64
A_HEADS = 4
IDX_HEADS = 4
IDX_DIM = 32
DSA_TOPK = 256
B_HEADS = 4
CMP_BLOCK = 32
CMP_STRIDE = 16
CMP_HIDDEN = 256
SLC_BLOCK = 64
SLC_TOPN = 16
WINDOW = 512
FORCE_SCORE = 1e9
C_HEADS = 4
C_VDIM = 2 * HEAD_DIM
N_BUCKETS = 32
MAX_DISTANCE = 128
TOP_K = 4
SWIGLU_LIMIT = 7.0
SWIGLU_ALPHA = 1.702
LN_EPS = 1e-5

LOG2E = math.log2(math.e)
Q_FOLD = HEAD_DIM ** -0.5 * LOG2E
NEG = -1e30
INT_MIN = -2 ** 31
NEG_INF_KEY = INT_MIN + 0x7FFFFF

LANES = 128
SUBLANES = 8
VMEM_LIMIT_BYTES = 56 * 1024 * 1024

T_ATT = 256
TM_PROJ = 512
TM_MOE = 512


def _cparams(sem):
    return pltpu.CompilerParams(dimension_semantics=sem, vmem_limit_bytes=VMEM_LIMIT_BYTES)


def _dot(a, b):
    return jnp.dot(a, b, preferred_element_type=F32)


def _dot_nt(a, b):
    return lax.dot_general(a, b, (((1,), (1,)), ((), ())), preferred_element_type=F32)


def _adaln_kernel(c_ref, w_ref, b_ref, o_ref):
    c = c_ref[...]
    a = (c * jax.nn.sigmoid(c)).astype(_MXU_DTYPE)
    o_ref[0] = _dot(a, w_ref[0].astype(_MXU_DTYPE)) + b_ref[0]


def _adaln(c, w, b):
    G, D, N = w.shape
    Bn = c.shape[0]
    tn = 768
    return pl.pallas_call(
        _adaln_kernel,
        grid=(G, N // tn),
        in_specs=[
            pl.BlockSpec((Bn, D), lambda g, j: (0, 0)),
            pl.BlockSpec((1, D, tn), lambda g, j: (g, 0, j)),
            pl.BlockSpec((1, 1, tn), lambda g, j: (g, 0, j)),
        ],
        out_specs=pl.BlockSpec((1, Bn, tn), lambda g, j: (g, 0, j)),
        out_shape=jax.ShapeDtypeStruct((G, Bn, N), F32),
        compiler_params=_cparams(("parallel", "parallel")),
        name="adaln",
    )(c, w, b.reshape(G, 1, N))


def _inproj_kernel(x_ref, mod_ref, wa_ref, wb_ref, wc_ref, ws_ref, wg_ref,
                   oa_ref, ob_ref, oc_ref, os_ref, og_ref):
    x = x_ref[0]
    shift = mod_ref[0, 0:1, :]
    scale = mod_ref[0, 1:2, :]
    h = (x * (1.0 + scale) + shift).astype(_MXU_DTYPE)
    for w_ref, o_ref in ((wa_ref, oa_ref), (wb_ref, ob_ref), (wc_ref, oc_ref),
                         (ws_ref, os_ref), (wg_ref, og_ref)):
        n = w_ref.shape[1]
        for j in range(0, n, 512):
            w = min(512, n - j)
            o_ref[0, :, j:j + w] = _dot(h, w_ref[:, j:j + w]).astype(o_ref.dtype)


def _inproj(x, mod, wa, wb, wc, ws, wg):
    Bn, S, D = x.shape
    tm = TM_PROJ
    ws_ = [wa, wb, wc, ws, wg]
    dts = [_MXU_DTYPE, _MXU_DTYPE, _MXU_DTYPE, F32, _MXU_DTYPE]
    return pl.pallas_call(
        _inproj_kernel,
        grid=(Bn, S // tm),
        in_specs=[pl.BlockSpec((1, tm, D), lambda b, i: (b, i, 0)),
                  pl.BlockSpec((1, 3, D), lambda b, i: (b, 0, 0))]
                 + [pl.BlockSpec(w.shape, lambda b, i: (0, 0)) for w in ws_],
        out_specs=[pl.BlockSpec((1, tm, w.shape[1]), lambda b, i: (b, i, 0)) for w in ws_],
        out_shape=[jax.ShapeDtypeStruct((Bn, S, w.shape[1]), dt) for w, dt in zip(ws_, dts)],
        compiler_params=_cparams(("parallel", "parallel")),
        name="inproj",
    )(x, mod, *ws_)


def _compress_kernel(r_ref, rs_ref, pa_ref, pb_ref, wa_ref, wb_ref, w2_ref, o_ref):
    r = r_ref[0].astype(F32)
    rs = rs_ref[0].astype(F32)
    outs = []
    for j in range(2):
        xa = (r + pa_ref[j]).astype(_MXU_DTYPE)
        xb = (rs + pb_ref[j]).astype(_MXU_DTYPE)
        hid = jax.nn.gelu(_dot(xa, wa_ref[j]) + _dot(xb, wb_ref[j]))
        outs.append(_dot(hid.astype(_MXU_DTYPE), w2_ref[j]))
    o_ref[0] = jnp.concatenate(outs, axis=1).astype(o_ref.dtype)


def _compress(r2, r2s, pa, pb, wa, wb, w2):
    Bn, NC, KW = r2.shape
    return pl.pallas_call(
        _compress_kernel,
        grid=(Bn,),
        in_specs=[pl.BlockSpec((1, NC, KW), lambda b: (b, 0, 0)),
                  pl.BlockSpec((1, NC, KW), lambda b: (b, 0, 0)),
                  pl.BlockSpec(pa.shape, lambda b: (0, 0, 0)),
                  pl.BlockSpec(pb.shape, lambda b: (0, 0, 0)),
                  pl.BlockSpec(wa.shape, lambda b: (0, 0, 0)),
                  pl.BlockSpec(wb.shape, lambda b: (0, 0, 0)),
                  pl.BlockSpec(w2.shape, lambda b: (0, 0, 0))],
        out_specs=pl.BlockSpec((1, NC, 2 * HEAD_DIM), lambda b: (b, 0, 0)),
        out_shape=jax.ShapeDtypeStruct((Bn, NC, 2 * HEAD_DIM), _MXU_DTYPE),
        compiler_params=_cparams(("parallel",)),
        name="nsa_compress",
    )(r2, r2s, pa, pb, wa, wb, w2)


def _online_t(s, m_ref, l_ref, acc_ref, v_t):
    m = m_ref[...]
    m_new = jnp.maximum(m, jnp.max(s, axis=0, keepdims=True))
    alpha = jnp.exp2(m - m_new)
    p = jnp.exp2(s - m_new)
    l_ref[...] = alpha * l_ref[...] + jnp.sum(p, axis=0, keepdims=True)
    acc_ref[...] = alpha * acc_ref[...] + _dot(v_t, p.astype(_MXU_DTYPE))
    m_ref[...] = m_new


def _reset(m_ref, l_ref, acc_ref):
    m_ref[...] = jnp.full(m_ref.shape, NEG, F32)
    l_ref[...] = jnp.zeros(l_ref.shape, F32)
    acc_ref[...] = jnp.zeros(acc_ref.shape, F32)


def _finish_t(m_ref, l_ref, acc_ref):
    return jnp.where(m_ref[...] > 0.5 * NEG, acc_ref[...] / jnp.maximum(l_ref[...], 1e-30), 0.0)


def _pair_loop(n, step, buf_a, buf_b):
    @pl.when(n > 0)
    def _():
        step(0, buf_a, None, None)

    def pair(i, carry):
        k0 = 2 * i
        step(jnp.minimum(k0 + 1, n - 1), buf_b, k0, buf_a)

        @pl.when(k0 + 1 < n)
        def _():
            step(jnp.minimum(k0 + 2, n - 1), buf_a, k0 + 1, buf_b)
        return carry

    lax.fori_loop(0, (n + 1) // 2, pair, 0)


def _near_chunks(qi, step, buf_a, buf_b):
    @pl.when(qi >= 1)
    def _():
        step(qi - 1, buf_a, None, None)
        step(qi, buf_b, qi - 1, buf_a, 1)

    @pl.when(qi == 0)
    def _():
        step(qi, buf_b, None, None)

    step(None, None, qi, buf_b, 0)


def _zero_extended_queries(q, n_heads, width):
    t = q.shape[0]
    pad = jnp.zeros((t, LANES - width), q.dtype)
    return jnp.concatenate(
        [jnp.concatenate([q[:, h * width:(h + 1) * width], pad], axis=1) for h in range(n_heads)],
        axis=0)


def _heads_to_rows_t(o_t, n_heads):
    t = o_t.shape[1] // n_heads
    stacked = jnp.concatenate([o_t[:, h * t:(h + 1) * t] for h in range(n_heads)], axis=0)
    return stacked.T


def _dsa_kernel(aq_ref, akv_ref, vt_ref, iq_ref, ik_ref, side_ref, bias_ref, cmask_ref, o_ref,
                keys_ref, hi_ref, lo_ref, acc_ref, m_ref, l_ref, run_ref, sa_ref, sb_ref, pa_ref,
                pb_ref, *, T, topk):
    qi = pl.program_id(1)
    nk = qi + 1
    H = A_HEADS

    q_z = _zero_extended_queries(aq_ref[0], H, HEAD_DIM)
    iq_z = _zero_extended_queries(iq_ref[0], IDX_HEADS, IDX_DIM)
    iw_t = side_ref[0].T[0:IDX_HEADS] * (IDX_HEADS ** -0.5 * IDX_DIM ** -0.5)

    def chunk_rows(kc):
        return pl.ds(pl.multiple_of(kc * T, T), T)

    def raw_scores(kc):
        return _dot_nt(ik_ref[0, chunk_rows(kc), :], iq_z)

    def to_keys(kc, raw, causal_mask):
        sc = jnp.maximum(raw, 0.0)
        isc = iw_t[0:1] * sc[:, 0:T]
        for h in range(1, IDX_HEADS):
            isc = isc + iw_t[h:h + 1] * sc[:, h * T:(h + 1) * T]
        isc = isc + 0.0
        if causal_mask is not None:
            isc = isc + causal_mask
        bits = pltpu.bitcast(isc, jnp.int32)
        keys = jnp.where(bits < 0, bits ^ 0x7FFFFFFF, bits)
        keys_ref[kc] = keys
        hi_ref[kc] = lax.shift_right_arithmetic(keys, 16).astype(jnp.int16)

    def far_keys(i, carry):
        k0 = 2 * i
        k1 = jnp.minimum(k0 + 1, qi - 1)
        raw0 = raw_scores(k0)
        raw1 = raw_scores(k1)
        to_keys(k0, raw0, None)
        to_keys(k1, raw1, None)
        return carry

    lax.fori_loop(0, (qi + 1) // 2, far_keys, 0)
    to_keys(qi, raw_scores(qi), cmask_ref[...])

    n_acc = 4
    half_rows = 2 * SUBLANES

    def count16(plane_ref, cand, strict):
        cand16 = cand.astype(jnp.int16)
        one, zero = jnp.int16(1), jnp.int16(0)

        def body(kc, accs):
            k = plane_ref[kc]
            hit = jnp.where(k > cand16 if strict else k >= cand16, one, zero)
            parts = hit.reshape(T // half_rows, half_rows, T)
            accs = list(accs)
            for r in range(T // half_rows):
                accs[r % n_acc] = accs[r % n_acc] + parts[r]
            return tuple(accs)

        accs = lax.fori_loop(0, nk, body,
                             tuple(jnp.zeros((half_rows, T), jnp.int16) for _ in range(n_acc)))
        total = (accs[0] + accs[1]) + (accs[2] + accs[3])
        return jnp.sum(total.astype(F32), axis=0, keepdims=True)

    def search16(plane_ref, base_count):
        def bit_step(b, thr):
            cand = thr + lax.shift_left(jnp.int32(1), 15 - b)
            cnt = base_count + count16(plane_ref, cand, False)
            return jnp.where(cnt >= float(topk), cand, thr)
        return lax.fori_loop(0, 16, bit_step, jnp.full((1, T), -2 ** 15, jnp.int32))

    thr_hi = search16(hi_ref, 0.0)
    above_hi = count16(hi_ref, thr_hi, True)

    def low_plane(kc, carry):
        keys = keys_ref[kc]
        lo = (keys & 0xFFFF) - 2 ** 15
        same = lax.shift_right_arithmetic(keys, 16) == thr_hi
        lo_ref[kc] = jnp.where(same, lo, -2 ** 15).astype(jnp.int16)
        return carry

    lax.fori_loop(0, nk, low_plane, 0)
    thr_lo = search16(lo_ref, above_hi)
    thr = thr_hi * 2 ** 16 + (thr_lo + 2 ** 15)
    need = float(topk) - (above_hi + count16(lo_ref, thr_lo, True))

    row = lax.broadcasted_iota(jnp.int32, (T, T), 0)
    col = lax.broadcasted_iota(jnp.int32, (T, T), 1)
    lower = jnp.where(col < row, 1.0, 0.0).astype(_MXU_DTYPE)

    _reset(m_ref, l_ref, acc_ref)
    run_ref[...] = jnp.zeros(run_ref.shape, F32)

    def step(k_next, bufs_next, k_cur, bufs_cur, delta=None):
        if k_cur is not None:
            s_cur, p_cur = bufs_cur
            keys = keys_ref[k_cur]
            eq = keys == thr
            run = run_ref[...]
            sel = ((keys > thr) | (eq & (p_cur[...] + run < need))) & (keys != NEG_INF_KEY)
            run_ref[...] = run + jnp.sum(jnp.where(eq, 1.0, 0.0), axis=0, keepdims=True)
            v_t = vt_ref[0, k_cur]
        if k_next is not None:
            s_next, p_next = bufs_next
            kch = akv_ref[0, chunk_rows(k_next), :]
        if k_next is not None:
            s_next[...] = _dot_nt(kch, q_z)
            eqf = jnp.where(keys_ref[k_next] == thr, 1.0, 0.0).astype(_MXU_DTYPE)
            p_next[...] = _dot(lower, eqf)
        if k_cur is not None:
            s = s_cur[...] if delta is None else s_cur[...] + bias_ref[delta]
            s = jnp.where(jnp.concatenate([sel] * H, axis=1), s, NEG)
            _online_t(s, m_ref, l_ref, acc_ref, v_t)

    _pair_loop(jnp.maximum(qi - 1, 0), step, (sa_ref, pa_ref), (sb_ref, pb_ref))
    _near_chunks(qi, step, (sa_ref, pa_ref), (sb_ref, pb_ref))

    o_ref[0] = _heads_to_rows_t(_finish_t(m_ref, l_ref, acc_ref), H).astype(o_ref.dtype)


def _dsa(apack, av_t, side, bias, cmask, T):
    Bn, S, _ = apack.shape
    topk = min(DSA_TOPK, S // 4)
    N = A_HEADS * T
    kern = functools.partial(_dsa_kernel, T=T, topk=topk)
    return pl.pallas_call(
        kern,
        grid=(Bn, S // T),
        in_specs=[pl.BlockSpec((1, T, 256), lambda b, i: (b, i, 0)),
                  pl.BlockSpec((1, S, 128), lambda b, i: (b, 0, 2)),
                  pl.BlockSpec((1,) + av_t.shape[1:], lambda b, i: (b, 0, 0, 0)),
                  pl.BlockSpec((1, T, 128), lambda b, i: (b, i, 3)),
                  pl.BlockSpec((1, S, 128), lambda b, i: (b, 0, 4)),
                  pl.BlockSpec((1, T, 128), lambda b, i: (b, i, 0)),
                  pl.BlockSpec(bias.shape, lambda b, i: (0, 0, 0)),
                  pl.BlockSpec(cmask.shape, lambda b, i: (0, 0))],
        out_specs=pl.BlockSpec((1, T, 256), lambda b, i: (b, i, 0)),
        out_shape=jax.ShapeDtypeStruct((Bn, S, 256), _MXU_DTYPE),
        scratch_shapes=[pltpu.VMEM((S // T, T, T), jnp.int32),
                        pltpu.VMEM((S // T, T, T), jnp.int16),
                        pltpu.VMEM((S // T, T, T), jnp.int16),
                        pltpu.VMEM((HEAD_DIM, N), F32),
                        pltpu.VMEM((1, N), F32), pltpu.VMEM((1, N), F32), pltpu.VMEM((1, T), F32),
                        pltpu.VMEM((T, N), F32), pltpu.VMEM((T, N), F32),
                        pltpu.VMEM((T, T), F32), pltpu.VMEM((T, T), F32)],
        compiler_params=_cparams(("parallel", "arbitrary")),
        name="dsa_attention",
    )(apack, apack, av_t, apack, apack, side, bias, cmask)


def _nsa_kernel(bq_ref, kvc_ref, vct_ref, kvs_ref, vst_ref, kvw_ref, vwt_ref, side_ref,
                bias_ref, wbias_ref, ovt_ref, o_ref, acc_ref, m_ref, l_ref, sa_ref, sb_ref,
                pa_ref, pb_ref, *, T, n_cmp, n_sel, n_win):
    qi = pl.program_id(1)
    nk = qi + 1
    H = B_HEADS
    NC = kvc_ref.shape[1]
    NS = ovt_ref.shape[0]
    log2_blk = int(math.log2(SLC_BLOCK))

    q_z = _zero_extended_queries(bq_ref[0], H, HEAD_DIM)
    t_q = qi * T + lax.broadcasted_iota(jnp.int32, (1, T), 1)
    t_st = jnp.concatenate([t_q] * H, axis=1)

    def chunk_rows(kc):
        return pl.ds(pl.multiple_of(kc * T, T), T)

    n_idx = lax.broadcasted_iota(jnp.int32, (NC, 1), 0)
    cvalid = (n_idx * CMP_STRIDE + (CMP_BLOCK - 1) <= t_st) & (n_idx < n_cmp)
    lc = jnp.where(cvalid, _dot_nt(kvc_ref[0], q_z), NEG)
    mc = jnp.max(lc, axis=0, keepdims=True)
    pc = jnp.where(cvalid, jnp.exp2(lc - mc), 0.0)
    pc = pc / jnp.maximum(jnp.sum(pc, axis=0, keepdims=True), 1e-30)
    o_cmp = _dot(vct_ref[0], pc.astype(_MXU_DTYPE))

    psum = pc[:, 0:T]
    for h in range(1, H):
        psum = psum + pc[:, h * T:(h + 1) * T]
    p_hi = psum.astype(_MXU_DTYPE)
    p_lo = (psum - p_hi.astype(F32)).astype(_MXU_DTYPE)
    imp = _dot(ovt_ref[...], p_hi) + _dot(ovt_ref[...], p_lo)

    blk = lax.broadcasted_iota(jnp.int32, (NS, T), 0)
    blkf = blk.astype(F32)
    cur = lax.shift_right_logical(t_q, log2_blk)
    forced = (blk == 0) | (blk == cur) | (blk == cur - 1)
    val = jnp.where(blk <= cur, jnp.where(forced, FORCE_SCORE, imp), -jnp.inf)

    def pick_one(_, carry):
        val, sel = carry
        mx = jnp.max(val, axis=0, keepdims=True)
        first = jnp.min(jnp.where(val == mx, blkf, float(NS)), axis=0, keepdims=True)
        pick = blkf == first
        sel = jnp.where(pick & (mx > -jnp.inf), 1.0, sel)
        return jnp.where(pick, -jnp.inf, val), sel

    _, sel = lax.fori_loop(0, n_sel, pick_one, (val, jnp.zeros((NS, T), F32)))
    sel = sel.astype(_MXU_DTYPE)

    e_row = lax.shift_right_logical(lax.broadcasted_iota(jnp.int32, (T, NS), 0), log2_blk)
    e_col = lax.broadcasted_iota(jnp.int32, (T, NS), 1)
    bpc = T // SLC_BLOCK

    _reset(m_ref, l_ref, acc_ref)

    def step(k_next, bufs_next, k_cur, bufs_cur, delta=None):
        if k_next is not None:
            s_next, p_next = bufs_next
            s_next[...] = _dot_nt(kvs_ref[0, chunk_rows(k_next), :], q_z)
            expand = jnp.where(e_col == k_next * bpc + e_row, 1.0, 0.0).astype(_MXU_DTYPE)
            p_next[...] = _dot(expand, sel)
        if k_cur is not None:
            s_cur, p_cur = bufs_cur
            addm = jnp.where(p_cur[...] > 0.5, 0.0, NEG)
            s = s_cur[...] + jnp.concatenate([addm] * H, axis=1)
            if delta is not None:
                s = s + bias_ref[delta]
            _online_t(s, m_ref, l_ref, acc_ref, vst_ref[0, k_cur])

    _pair_loop(jnp.maximum(qi - 1, 0), step, (sa_ref, pa_ref), (sb_ref, pb_ref))
    _near_chunks(qi, step, (sa_ref, pa_ref), (sb_ref, pb_ref))
    o_slc = _finish_t(m_ref, l_ref, acc_ref)

    _reset(m_ref, l_ref, acc_ref)

    def produce_win(delta, buf):
        buf[...] = _dot_nt(kvw_ref[0, chunk_rows(jnp.maximum(qi - delta, 0)), :], q_z)

    def attend_win(delta, buf):
        kc = qi - delta
        s = buf[...] + wbias_ref[delta] + jnp.where(kc >= 0, 0.0, NEG)
        _online_t(s, m_ref, l_ref, acc_ref, vwt_ref[0, jnp.maximum(kc, 0)])

    bufs = (sa_ref, sb_ref)
    produce_win(0, bufs[0])
    for delta in range(n_win):
        if delta + 1 < n_win:
            produce_win(delta + 1, bufs[(delta + 1) % 2])
        attend_win(delta, bufs[delta % 2])
    o_win = _finish_t(m_ref, l_ref, acc_ref)

    g = jax.nn.sigmoid(side_ref[0].T[IDX_HEADS:IDX_HEADS + 3 * H])
    outs = []
    for h in range(H):
        sl = slice(h * T, (h + 1) * T)
        outs.append(g[3 * h:3 * h + 1] * o_cmp[:, sl] + g[3 * h + 1:3 * h + 2] * o_slc[:, sl]
                    + g[3 * h + 2:3 * h + 3] * o_win[:, sl])
    o_ref[0] = jnp.concatenate(outs, axis=0).T.astype(o_ref.dtype)


def _nsa(bpack, kvc, vc_t, vs_t, vw_t, side, bias, wbias, overlap_t, T, n_cmp, n_sel):
    Bn, S, _ = bpack.shape
    NC = kvc.shape[1]
    N = B_HEADS * T
    kern = functools.partial(_nsa_kernel, T=T, n_cmp=n_cmp, n_sel=n_sel, n_win=wbias.shape[0])
    vtspec = lambda a: pl.BlockSpec((1,) + a.shape[1:], lambda b, i: (b, 0, 0, 0))
    return pl.pallas_call(
        kern,
        grid=(Bn, S // T),
        in_specs=[pl.BlockSpec((1, T, 256), lambda b, i: (b, i, 0)),
                  pl.BlockSpec((1, NC, 128), lambda b, i: (b, 0, 0)),
                  pl.BlockSpec((1, HEAD_DIM, NC), lambda b, i: (b, 0, 0)),
                  pl.BlockSpec((1, S, 128), lambda b, i: (b, 0, 3)),
                  vtspec(vs_t),
                  pl.BlockSpec((1, S, 128), lambda b, i: (b, 0, 4)),
                  vtspec(vw_t),
                  pl.BlockSpec((1, T, 128), lambda b, i: (b, i, 0)),
                  pl.BlockSpec(bias.shape, lambda b, i: (0, 0, 0)),
                  pl.BlockSpec(wbias.shape, lambda b, i: (0, 0, 0)),
                  pl.BlockSpec(overlap_t.shape, lambda b, i: (0, 0))],
        out_specs=pl.BlockSpec((1, T, 256), lambda b, i: (b, i, 0)),
        out_shape=jax.ShapeDtypeStruct((Bn, S, 256), _MXU_DTYPE),
        scratch_shapes=[pltpu.VMEM((HEAD_DIM, N), F32),
                        pltpu.VMEM((1, N), F32), pltpu.VMEM((1, N), F32),
                        pltpu.VMEM((T, N), F32), pltpu.VMEM((T, N), F32),
                        pltpu.VMEM((T, T), F32), pltpu.VMEM((T, T), F32)],
        compiler_params=_cparams(("parallel", "arbitrary")),
        name="nsa_attention",
    )(bpack, kvc, vc_t, bpack, vs_t, bpack, vw_t, side, bias, wbias, overlap_t)


def _diff_kernel(cq_ref, ck_ref, vt_ref, bias_ref, dl_ref, g_ref, o_ref, acc_ref, m_ref, l_ref,
                 sa_ref, sb_ref, *, T, lam_init):
    qi = pl.program_id(1)
    nk = qi + 1
    H = C_HEADS
    dl = dl_ref[...]
    lam = (jnp.exp(jnp.sum(dl[0:1] * dl[1:2], axis=1, keepdims=True))
           - jnp.exp(jnp.sum(dl[2:3] * dl[3:4], axis=1, keepdims=True)) + lam_init)

    cq = cq_ref[0]
    lane = lax.broadcasted_iota(jnp.int32, (T, 2 * HEAD_DIM), 1)
    q_z = []
    for h in range(H):
        qh = cq[:, h * 2 * HEAD_DIM:(h + 1) * 2 * HEAD_DIM]
        zero = jnp.zeros_like(qh)
        q_z.append(jnp.concatenate([jnp.where(lane < HEAD_DIM, qh, zero),
                                    jnp.where(lane >= HEAD_DIM, qh, zero)], axis=0))

    for h in range(H):
        _reset(m_ref.at[h], l_ref.at[h], acc_ref.at[h])

    def step(k_next, dst, k_cur, src, delta=None):
        if k_next is not None:
            kch = ck_ref[0, pl.ds(pl.multiple_of(k_next * T, T), T), :]
        if k_cur is not None:
            v_t = vt_ref[0, k_cur]
        for h in range(H):
            if k_next is not None:
                dst[h] = _dot_nt(kch[:, h * 2 * HEAD_DIM:(h + 1) * 2 * HEAD_DIM], q_z[h])
            if k_cur is not None:
                s = src[h]
                if delta is not None:
                    b = bias_ref[delta, h]
                    s = jnp.concatenate([s[:, 0:T] + b, s[:, T:2 * T] + b], axis=1)
                _online_t(s, m_ref.at[h], l_ref.at[h], acc_ref.at[h],
                          v_t[h * C_VDIM:(h + 1) * C_VDIM])

    _pair_loop(jnp.maximum(qi - 1, 0), step, sa_ref, sb_ref)
    _near_chunks(qi, step, sa_ref, sb_ref)

    outs = []
    for h in range(H):
        o = _finish_t(m_ref.at[h], l_ref.at[h], acc_ref.at[h])
        oc = o[:, 0:T] - lam * o[:, T:2 * T]
        rms = lax.rsqrt(jnp.mean(jnp.square(oc), axis=0, keepdims=True) + LN_EPS)
        outs.append(oc * rms * g_ref[...] * (1.0 - lam_init))
    o_ref[0] = jnp.concatenate(outs, axis=0).T.astype(o_ref.dtype)


def _diff(cpack, cv_t, bias, dl, g, T, lam_init):
    Bn, S, _ = cpack.shape
    W = C_HEADS * C_VDIM
    kern = functools.partial(_diff_kernel, T=T, lam_init=lam_init)
    return pl.pallas_call(
        kern,
        grid=(Bn, S // T),
        in_specs=[pl.BlockSpec((1, T, W), lambda b, i: (b, i, 0)),
                  pl.BlockSpec((1, S, W), lambda b, i: (b, 0, 1)),
                  pl.BlockSpec((1,) + cv_t.shape[1:], lambda b, i: (b, 0, 0, 0)),
                  pl.BlockSpec(bias.shape, lambda b, i: (0, 0, 0, 0)),
                  pl.BlockSpec(dl.shape, lambda b, i: (0, 0)),
                  pl.BlockSpec(g.shape, lambda b, i: (0, 0))],
        out_specs=pl.BlockSpec((1, T, W), lambda b, i: (b, i, 0)),
        out_shape=jax.ShapeDtypeStruct((Bn, S, W), _MXU_DTYPE),
        scratch_shapes=[pltpu.VMEM((C_HEADS, C_VDIM, 2 * T), F32),
                        pltpu.VMEM((C_HEADS, 1, 2 * T), F32),
                        pltpu.VMEM((C_HEADS, 1, 2 * T), F32),
                        pltpu.VMEM((C_HEADS, T, 2 * T), F32),
                        pltpu.VMEM((C_HEADS, T, 2 * T), F32)],
        compiler_params=_cparams(("parallel", "arbitrary")),
        name="diff_attention",
    )(cpack, cpack, cv_t, bias, dl, g)


def _layer_norm(v, g, b):
    mu = jnp.mean(v, axis=1, keepdims=True)
    d = v - mu
    var = jnp.mean(jnp.square(d), axis=1, keepdims=True)
    return d * lax.rsqrt(var + LN_EPS) * g + b


def _split_hi_lo(v):
    hi = v.astype(_MXU_DTYPE)
    lo = (v - hi.astype(F32)).astype(_MXU_DTYPE)
    return hi, lo


def _outproj_kernel(oa_ref, ob_ref, oc_ref, mg_ref, x_ref, mod1_ref, mod2_ref,
                    wa_ref, wb_ref, wc_ref, wo_ref, ln_ref, rw_ref, rb_ref,
                    x1_ref, h2_ref, tr_ref, tw_ref, cnt_ref, run_ref, *, dn_alpha, n_experts):
    D = x_ref.shape[2]
    tm = x_ref.shape[1]
    first_step = (pl.program_id(0) == 0) & (pl.program_id(1) == 0)

    @pl.when(first_step)
    def _():
        run_ref[...] = jnp.zeros(run_ref.shape, F32)

    ya = _dot(oa_ref[0], wa_ref[...])
    yb = _dot(ob_ref[0], wb_ref[...])
    yc = _dot(oc_ref[0], wc_ref[...])
    mg = mg_ref[0].astype(F32)
    merged = (jax.nn.sigmoid(mg[:, 0:D]) * ya + jax.nn.sigmoid(mg[:, D:2 * D]) * yb
              + jax.nn.sigmoid(mg[:, 2 * D:3 * D]) * yc)
    y = _dot(merged.astype(_MXU_DTYPE), wo_ref[...])
    gate1 = mod1_ref[0, 2:3, :]
    x1 = _layer_norm(dn_alpha * x_ref[0] + (1.0 + gate1) * y, ln_ref[0:1, :], ln_ref[1:2, :])
    x1_ref[0] = x1
    h2 = x1 * (1.0 + mod2_ref[0, 1:2, :]) + mod2_ref[0, 0:1, :]
    h2_ref[0] = h2.astype(h2_ref.dtype)

    h_hi, h_lo = _split_hi_lo(h2)
    w_hi, w_lo = _split_hi_lo(rw_ref[...])
    hw = _dot(h_hi, jnp.concatenate([w_hi, w_lo], axis=1))
    logits = hw[:, 0:LANES] + hw[:, LANES:2 * LANES] + _dot(h_lo, w_hi) + rb_ref[...]
    lane = lax.broadcasted_iota(jnp.int32, logits.shape, 1)
    lanef = lane.astype(F32)
    work = jnp.where(lane < n_experts, logits, -jnp.inf)
    tv = jnp.zeros(logits.shape, F32)
    firsts, onehots = [], []
    v0 = None
    for k in range(TOP_K):
        mx = jnp.max(work, axis=1, keepdims=True)
        first = jnp.min(jnp.where(work == mx, lanef, float(LANES)), axis=1, keepdims=True)
        if k == 0:
            v0 = mx
        tv = jnp.where(lane == k, jnp.exp(mx - v0), tv)
        hit = lanef == first
        work = jnp.where(hit, -jnp.inf, work)
        firsts.append(first)
        onehots.append(jnp.where(hit, 1.0, 0.0))
    tw_ref[0] = tv / jnp.sum(tv, axis=1, keepdims=True)

    chosen = onehots[0] + onehots[1] + onehots[2] + onehots[3]
    r_i = lax.broadcasted_iota(jnp.int32, (tm, tm), 0)
    c_i = lax.broadcasted_iota(jnp.int32, (tm, tm), 1)
    earlier = jnp.where(c_i < r_i, 1.0, 0.0).astype(_MXU_DTYPE)
    base = _dot(earlier, chosen.astype(_MXU_DTYPE)) + run_ref[0:1, :]
    tr = jnp.zeros(logits.shape, F32)
    for k in range(TOP_K):
        rank = jnp.sum(onehots[k] * base, axis=1, keepdims=True)
        tr = jnp.where(lane == k, firsts[k], tr)
        tr = jnp.where(lane == TOP_K + k, rank, tr)
    tr_ref[0] = tr.astype(jnp.int32)
    run_ref[...] = run_ref[...] + jnp.sum(chosen, axis=0, keepdims=True)
    cnt_ref[...] = run_ref[...]


def _outproj(oa, ob, oc, mg, x, mod1, mod2, wa, wb, wc, wo, ln, rw, rb, dn_alpha, n_experts):
    Bn, S, D = x.shape
    tm = TM_PROJ
    row = lambda w: pl.BlockSpec((1, tm, w), lambda b, i: (b, i, 0))
    full = lambda a: pl.BlockSpec(a.shape, lambda b, i: (0,) * a.ndim)
    modspec = pl.BlockSpec((1, 3, D), lambda b, i: (b, 0, 0))
    kern = functools.partial(_outproj_kernel, dn_alpha=dn_alpha, n_experts=n_experts)
    return pl.pallas_call(
        kern,
        grid=(Bn, S // tm),
        in_specs=[row(oa.shape[2]), row(ob.shape[2]), row(oc.shape[2]), row(mg.shape[2]), row(D),
                  modspec, modspec, full(wa), full(wb), full(wc), full(wo), full(ln), full(rw),
                  full(rb)],
        out_specs=[row(D), row(D), row(LANES), row(LANES),
                   pl.BlockSpec((SUBLANES, LANES), lambda b, i: (0, 0))],
        out_shape=[jax.ShapeDtypeStruct((Bn, S, D), F32),
                   jax.ShapeDtypeStruct((Bn, S, D), _MXU_DTYPE),
                   jax.ShapeDtypeStruct((Bn, S, LANES), jnp.int32),
                   jax.ShapeDtypeStruct((Bn, S, LANES), F32),
                   jax.ShapeDtypeStruct((SUBLANES, LANES), F32)],
        scratch_shapes=[pltpu.VMEM((SUBLANES, LANES), F32)],
        compiler_params=_cparams(("arbitrary", "arbitrary")),
        name="merge_outproj_ln_router",
    )(oa, ob, oc, mg, x, mod1, mod2, wa, wb, wc, wo, ln, rw, rb)


def _deinterleave_kernel(w_ref, p_ref, og_ref, ou_ref):
    n = w_ref.shape[1]
    for j in range(n // (2 * LANES)):
        blk = w_ref[:, j * 2 * LANES:(j + 1) * 2 * LANES].astype(_MXU_DTYPE)
        r = _dot(blk, p_ref[...])
        og_ref[:, j * LANES:(j + 1) * LANES] = r[:, 0:LANES].astype(og_ref.dtype)
        ou_ref[:, j * LANES:(j + 1) * LANES] = r[:, LANES:2 * LANES].astype(ou_ref.dtype)


def _deinterleave(w_all, part, n_parts):
    N2 = w_all.shape[1]
    R = w_all.shape[0] // n_parts
    tr = 1024
    first = part * (R // tr)
    perm = np.zeros((2 * LANES, 2 * LANES), np.float32)
    perm[2 * np.arange(LANES), np.arange(LANES)] = 1.0
    perm[2 * np.arange(LANES) + 1, LANES + np.arange(LANES)] = 1.0
    return pl.pallas_call(
        _deinterleave_kernel,
        grid=(R // tr,),
        in_specs=[pl.BlockSpec((tr, N2), lambda i: (first + i, 0)),
                  pl.BlockSpec(perm.shape, lambda i: (0, 0))],
        out_specs=[pl.BlockSpec((tr, N2 // 2), lambda i: (i, 0))] * 2,
        out_shape=[jax.ShapeDtypeStruct((R, N2 // 2), _MXU_DTYPE)] * 2,
        compiler_params=_cparams(("parallel",)),
        name="expert_weight_deinterleave",
    )(w_all, jnp.asarray(perm).astype(_MXU_DTYPE))


def _moe_kernel(te_ref, tv_ref, x_ref, wg_ref, wu_ref, wd_ref, bg_ref, bu_ref, bd_ref, o_ref):
    i = pl.program_id(0)

    @pl.when(tv_ref[i] > 0)
    def _():
        x = x_ref[...]
        dff = wg_ref.shape[2]
        ck = 512
        y = jnp.zeros(o_ref.shape, F32)
        for j in range(0, dff, ck):
            gate = jnp.minimum(_dot(x, wg_ref[0, :, j:j + ck]) + bg_ref[0, :, j:j + ck], SWIGLU_LIMIT)
            up = jnp.clip(_dot(x, wu_ref[0, :, j:j + ck]) + bu_ref[0, :, j:j + ck],
                          -SWIGLU_LIMIT, SWIGLU_LIMIT)
            act = (up + 1.0) * (gate * jax.nn.sigmoid(SWIGLU_ALPHA * gate))
            y = y + _dot(act.astype(_MXU_DTYPE), wd_ref[0, j:j + ck, :])
        o_ref[...] = (y + bd_ref[0]).astype(o_ref.dtype)


def _moe(xs, tile_expert, tile_valid, wg, wu, wd_all, wd_first, bg, bu, bd):
    R, D = xs.shape
    tm = TM_MOE
    nt = R // tm
    wspec = lambda a: pl.BlockSpec((1,) + a.shape[1:], lambda i, te, tv: (te[i], 0, 0))
    wdspec = pl.BlockSpec((1,) + wd_all.shape[1:], lambda i, te, tv: (wd_first + te[i], 0, 0))
    grid_spec = pltpu.PrefetchScalarGridSpec(
        num_scalar_prefetch=2,
        grid=(nt,),
        in_specs=[pl.BlockSpec((tm, D), lambda i, te, tv: (i, 0)),
                  wspec(wg), wspec(wu), wdspec, wspec(bg), wspec(bu), wspec(bd)],
        out_specs=pl.BlockSpec((tm, D), lambda i, te, tv: (i, 0)),
    )
    return pl.pallas_call(
        _moe_kernel,
        grid_spec=grid_spec,
        out_shape=jax.ShapeDtypeStruct((R, D), _MXU_DTYPE),
        compiler_params=_cparams(("arbitrary",)),
        name="moe_grouped_mlp",
    )(tile_expert, tile_valid, xs, wg, wu, wd_all, bg, bu, bd)


def _combine_kernel(yg_ref, tw_ref, x_ref, mod_ref, ln_ref, o_ref, *, dn_alpha):
    tw = tw_ref[0]
    y = tw[:, 0:1] * yg_ref[0, 0].astype(F32)
    for k in range(1, TOP_K):
        y = y + tw[:, k:k + 1] * yg_ref[k, 0].astype(F32)
    gate = mod_ref[0, 2:3, :]
    o_ref[0] = _layer_norm(dn_alpha * x_ref[0] + (1.0 + gate) * y, ln_ref[0:1, :], ln_ref[1:2, :])


def _combine(yg, tw, x, mod, ln, dn_alpha):
    Bn, S, D = x.shape
    tm = TM_PROJ
    row = lambda w: pl.BlockSpec((1, tm, w), lambda b, i: (b, i, 0))
    return pl.pallas_call(
        functools.partial(_combine_kernel, dn_alpha=dn_alpha),
        grid=(Bn, S // tm),
        in_specs=[pl.BlockSpec((TOP_K, 1, tm, D), lambda b, i: (0, b, i, 0)), row(LANES), row(D),
                  pl.BlockSpec((1, 3, D), lambda b, i: (b, 0, 0)),
                  pl.BlockSpec(ln.shape, lambda b, i: (0, 0))],
        out_specs=row(D),
        out_shape=jax.ShapeDtypeStruct((Bn, S, D), F32),
        compiler_params=_cparams(("parallel", "parallel")),
        name="moe_combine_ln",
    )(yg, tw, x, mod, ln)


def _bucket_np(dist):
    exact = N_BUCKETS // 2
    n = np.maximum(dist, 0)
    nf = np.maximum(n, 1).astype(np.float32)
    large = exact + (np.log(nf / exact) / math.log(MAX_DISTANCE / exact)
                     * (N_BUCKETS - exact)).astype(np.int32)
    large = np.minimum(large, N_BUCKETS - 1)
    return np.where(n < exact, n, large)


def _bias_tiles_t(tab, T, n_delta, window=None, shift_far=True):
    H = tab.shape[1]
    L = 2 * T
    k = np.arange(L)
    diff = np.where(k < T, k, k - L)
    d = diff[None, :] + (np.arange(n_delta) * T)[:, None]
    ok = d >= 0 if window is None else (d >= 0) & (d < window)
    onehot = np.eye(N_BUCKETS, dtype=np.float32)[_bucket_np(d)]
    prof = jnp.einsum("dlb,bh->dhl", jnp.asarray(onehot), tab, precision=lax.Precision.HIGHEST)
    if shift_far:
        prof = prof - tab[N_BUCKETS - 1][None, :, None]
    prof = jnp.where(jnp.asarray(ok)[:, None, :], prof * LOG2E, NEG)
    rolled = jnp.tile(prof, (1, 1, T))[:, :, :T * (L - 1)].reshape(n_delta, H, T, L - 1)
    return rolled[:, :, :, :T]


def _stack_tiles(tiles):
    n, H, T, _ = tiles.shape
    return jnp.transpose(tiles, (0, 2, 1, 3)).reshape(n, T, H * T)


def _slc_overlap_np(n_cmp, n_slc, rows):
    start = np.arange(n_cmp) * CMP_STRIDE
    end = start + CMP_BLOCK
    bs = np.arange(n_slc) * SLC_BLOCK
    ov = (start[:, None] < bs[None, :] + SLC_BLOCK) & (end[:, None] > bs[None, :])
    out = np.zeros((rows, n_slc), np.float32)
    out[:n_cmp] = ov
    return out


def _chunked_t(v, T):
    Bn, S, d = v.shape
    return jnp.transpose(v.reshape(Bn, S // T, T, d), (0, 1, 3, 2))


def _routing(top_i, rank, counts, tm):
    N = top_i.shape[0]
    E = counts.shape[0]
    A = N * TOP_K
    nt = A // tm + E
    padded = ((counts + tm - 1) // tm) * tm
    starts = jnp.cumsum(counts) - counts
    pstarts = jnp.cumsum(padded) - padded
    pend = pstarts + padded
    tile_start = jnp.arange(nt, dtype=jnp.int32) * tm
    tile_valid = (tile_start < pend[-1]).astype(jnp.int32)
    te = jnp.minimum(jnp.sum((tile_start[:, None] >= pend[None, :]).astype(jnp.int32), axis=1), E - 1)
    last_valid = jnp.max(jnp.where(tile_valid > 0, te, 0))
    tile_expert = jnp.where(tile_valid > 0, te, last_valid).astype(jnp.int32)
    onehot_t = tile_expert[:, None] == jnp.arange(E)[None, :]
    pick = lambda tab: jnp.sum(jnp.where(onehot_t, tab[None, :], 0), axis=1)
    order = jnp.argsort(top_i.reshape(A), stable=True).astype(jnp.int32)
    in_group = (tile_start - pick(pstarts))[:, None] + jnp.arange(tm, dtype=jnp.int32)[None, :]
    ok = (in_group < pick(counts)[:, None]) & (tile_valid[:, None] > 0)
    slot = jnp.clip(pick(starts)[:, None] + in_group, 0, A - 1)
    filler = (jnp.arange(nt * tm, dtype=jnp.int32) % N).reshape(nt, tm)
    row_tok = jnp.where(ok, jnp.take(order, slot.reshape(-1), mode="clip").reshape(nt, tm) // TOP_K,
                        filler)
    onehot_a = top_i[:, :, None] == jnp.arange(E)[None, None, :]
    pos = rank + jnp.sum(jnp.where(onehot_a, pstarts[None, None, :], 0), axis=2)
    return tile_expert, tile_valid, row_tok.reshape(nt * tm), pos.astype(jnp.int32)


def kernel(x, c, rel_bias, mod_attn_w, mod_attn_b, w_in, cmp_pos, cmp_w1, cmp_w2, diff_lambda,
           diff_norm_g, w_branch_a, w_branch_b, w_branch_c, w_out, ln1_g, ln1_b, mod_ffn_w,
           mod_ffn_b, router_w, router_b, exp_w_gu, exp_b_gu, exp_w_down, exp_b_down, ln2_g, ln2_b):
    Bn, S, D = x.shape
    L = w_in.shape[0]
    E = exp_w_gu.shape[1]
    T = T_ATT
    cdt = _MXU_DTYPE
    dn_alpha = (2 * L) ** 0.25
    assert S % T == 0 and S % TM_PROJ == 0 and (Bn * S * TOP_K) % TM_MOE == 0
    assert (S // SLC_BLOCK) % SUBLANES == 0 and E <= LANES and T >= MAX_DISTANCE

    mod_w = jnp.stack([mod_attn_w, mod_ffn_w], axis=1).reshape(2 * L, D, 3 * D)
    mod_b = jnp.stack([mod_attn_b, mod_ffn_b], axis=1).reshape(2 * L, 3 * D)
    mods = _adaln(c, mod_w, mod_b).reshape(2 * L, Bn, 3, D)

    n_win = WINDOW // T + 1
    tab_b = rel_bias[:, A_HEADS:A_HEADS + B_HEADS]
    bias_a = _stack_tiles(_bias_tiles_t(rel_bias[:, :A_HEADS], T, 2))
    bias_b = _stack_tiles(_bias_tiles_t(tab_b, T, 2))
    bias_w = _stack_tiles(_bias_tiles_t(tab_b, T, n_win, WINDOW, shift_far=False))
    bias_c = _bias_tiles_t(rel_bias[:, A_HEADS + B_HEADS:], T, 2)
    jj = np.arange(T)
    cmask = jnp.asarray(np.where(jj[None, :] >= jj[:, None], 0.0, -np.inf).astype(np.float32))

    n_cmp = (S - CMP_BLOCK) // CMP_STRIDE + 1
    n_slc = S // SLC_BLOCK
    n_sel = min(SLC_TOPN, n_slc)
    NC = S // CMP_STRIDE
    overlap_t = jnp.asarray(_slc_overlap_np(n_cmp, n_slc, NC).T).astype(cdt)

    rows16 = CMP_STRIDE * 2 * HEAD_DIM
    w_gu_all = exp_w_gu.reshape(L * E * D, -1)
    w_down_all = exp_w_down.astype(cdt).reshape(L * E, exp_w_down.shape[2], D)

    for l in range(L):
        lam_init = 0.8 - 0.6 * math.exp(-0.3 * l)
        wl = w_in[l]
        wa = jnp.pad(jnp.concatenate([wl[:, 0:256] * Q_FOLD, wl[:, 256:544]], axis=1),
                     ((0, 0), (0, 96))).astype(cdt)
        wb = jnp.concatenate([wl[:, 548:804] * Q_FOLD, wl[:, 804:1188]], axis=1).astype(cdt)
        wc = jnp.concatenate([wl[:, 1200:1712] * Q_FOLD, wl[:, 1712:2736]], axis=1).astype(cdt)
        ws = jnp.pad(jnp.concatenate([wl[:, 544:548], wl[:, 1188:1200]], axis=1),
                     ((0, 0), (0, LANES - 16))).astype(cdt)
        wg = wl[:, 2736:].astype(cdt)

        apack, bpack, cpack, side, mg = _inproj(x, mods[2 * l], wa, wb, wc, ws, wg)

        r2 = bpack[:, :, 256:384].reshape(Bn, NC, rows16)
        r2s = jnp.concatenate([r2[:, 1:], jnp.zeros_like(r2[:, :1])], axis=1)
        half = CMP_BLOCK // 2
        zpad = jnp.zeros((half, HEAD_DIM), F32)
        pa, pb, wca, wcb = [], [], [], []
        for j in range(2):
            pos = cmp_pos[l, j]
            w1 = cmp_w1[l, j].reshape(CMP_BLOCK, HEAD_DIM, CMP_HIDDEN)
            zw = jnp.zeros((half, HEAD_DIM, CMP_HIDDEN), F32)
            kv = (lambda a, z: jnp.concatenate([a, z], axis=1)) if j == 0 else \
                 (lambda a, z: jnp.concatenate([z, a], axis=1))
            pa.append(kv(pos[:half], zpad).reshape(1, rows16))
            pb.append(kv(pos[half:], zpad).reshape(1, rows16))
            wca.append(kv(w1[:half], zw).reshape(rows16, CMP_HIDDEN))
            wcb.append(kv(w1[half:], zw).reshape(rows16, CMP_HIDDEN))
        kvc = _compress(r2, r2s, jnp.stack(pa), jnp.stack(pb), jnp.stack(wca).astype(cdt),
                        jnp.stack(wcb).astype(cdt), cmp_w2[l].astype(cdt))

        oa = _dsa(apack, _chunked_t(apack[:, :, 320:384], T), side, bias_a, cmask, T)
        ob = _nsa(bpack, kvc, jnp.transpose(kvc[:, :, HEAD_DIM:], (0, 2, 1)),
                  _chunked_t(bpack[:, :, 448:512], T), _chunked_t(bpack[:, :, 576:640], T),
                  side, bias_b, bias_w, overlap_t, T, n_cmp, n_sel)
        oc = _diff(cpack, _chunked_t(cpack[:, :, 1024:1536], T), bias_c, diff_lambda[l],
                   diff_norm_g[l].reshape(C_VDIM, 1), T, lam_init)

        rw = jnp.pad(router_w[l], ((0, 0), (0, LANES - E)))
        rb = jnp.pad(router_b[l], (0, LANES - E)).reshape(1, LANES)
        x1, h2, tr, tw, cnt = _outproj(
            oa, ob, oc, mg, x, mods[2 * l], mods[2 * l + 1],
            w_branch_a[l].astype(cdt), w_branch_b[l].astype(cdt), w_branch_c[l].astype(cdt),
            w_out[l].astype(cdt), jnp.stack([ln1_g[l], ln1_b[l]]), rw, rb, dn_alpha, E)

        N = Bn * S
        tr = tr.reshape(N, LANES)
        tile_expert, tile_valid, row_tok, pos = _routing(
            tr[:, 0:TOP_K], tr[:, TOP_K:2 * TOP_K], cnt[0, :E].astype(jnp.int32), TM_MOE)
        xs = jnp.take(h2.reshape(N, D), row_tok, axis=0, mode="clip")
        wgate, wup = _deinterleave(w_gu_all, l, L)
        bgu = exp_b_gu[l].reshape(E, 1, -1, 2)
        ys = _moe(xs, tile_expert, tile_valid,
                  wgate.reshape(E, D, -1), wup.reshape(E, D, -1), w_down_all, l * E,
                  bgu[..., 0], bgu[..., 1], exp_b_down[l].reshape(E, 1, D))
        yg = jnp.take(ys, pos.T.reshape(-1), axis=0, mode="clip").reshape(TOP_K, Bn, S, D)
        x = _combine(yg, tw, x1, mods[2 * l + 1], jnp.stack([ln2_g[l], ln2_b[l]]), dn_alpha)
    return x
```

```python
import functools
import math

import numpy as np
import jax
import jax.numpy as jnp
from jax import lax
from jax.experimental import pallas as pl
from jax.experimental.pallas import tpu as pltpu

F32 = jnp.float32
_MXU_DTYPE = jnp.bfloat16

HEAD_DIM = 64
A_HEADS = 4
IDX_HEADS = 4
IDX_DIM = 32
DSA_TOPK = 256
B_HEADS = 4
CMP_BLOCK = 32
CMP_STRIDE = 16
CMP_HIDDEN = 256
SLC_BLOCK = 64
SLC_TOPN = 16
WINDOW = 512
FORCE_SCORE = 1e9
C_HEADS = 4
C_VDIM = 2 * HEAD_DIM
N_BUCKETS = 32
MAX_DISTANCE = 128
TOP_K = 4
SWIGLU_LIMIT = 7.0
SWIGLU_ALPHA = 1.702
LN_EPS = 1e-5

LOG2E = math.log2(math.e)
Q_FOLD = HEAD_DIM ** -0.5 * LOG2E
NEG = -1e30
INT_MIN = -2 ** 31
NEG_INF_KEY = INT_MIN + 0x7FFFFF

LANES = 128
SUBLANES = 8
VMEM_LIMIT_BYTES = 56 * 1024 * 1024

T_ATT = 256
TM_PROJ = 512
TM_MOE = 512


def _cparams(sem):
    return pltpu.CompilerParams(dimension_semantics=sem, vmem_limit_bytes=VMEM_LIMIT_BYTES)


def _dot(a, b):
    return jnp.dot(a, b, preferred_element_type=F32)


def _dot_nt(a, b):
    return lax.dot_general(a, b, (((1,), (1,)), ((), ())), preferred_element_type=F32)


def _adaln_kernel(c_ref, w_ref, b_ref, o_ref):
    c = c_ref[...]
    a = (c * jax.nn.sigmoid(c)).astype(_MXU_DTYPE)
    o_ref[0] = _dot(a, w_ref[0].astype(_MXU_DTYPE)) + b_ref[0]


def _adaln(c, w, b):
    G, D, N = w.shape
    Bn = c.shape[0]
    tn = 768
    return pl.pallas_call(
        _adaln_kernel,
        grid=(G, N // tn),
        in_specs=[
            pl.BlockSpec((Bn, D), lambda g, j: (0, 0)),
            pl.BlockSpec((1, D, tn), lambda g, j: (g, 0, j)),
            pl.BlockSpec((1, 1, tn), lambda g, j: (g, 0, j)),
        ],
        out_specs=pl.BlockSpec((1, Bn, tn), lambda g, j: (g, 0, j)),
        out_shape=jax.ShapeDtypeStruct((G, Bn, N), F32),
        compiler_params=_cparams(("parallel", "parallel")),
        name="adaln",
    )(c, w, b.reshape(G, 1, N))


_T_COLS = ((0, 256, 128), (1, 384, 128), (1, 512, 128), (2, 1024, 512))


def _inproj_kernel(x_ref, mod_ref, wa_ref, wb_ref, wc_ref, ws_ref, wg_ref,
                   oa_ref, ob_ref, oc_ref, os_ref, og_ref, *t_refs):
    x = x_ref[0]
    tm = x.shape[0]
    shift = mod_ref[0, 0:1, :]
    scale = mod_ref[0, 1:2, :]
    h = (x * (1.0 + scale) + shift).astype(_MXU_DTYPE)
    for g, (w_ref, o_ref) in enumerate(((wa_ref, oa_ref), (wb_ref, ob_ref), (wc_ref, oc_ref),
                                        (ws_ref, os_ref), (wg_ref, og_ref))):
        n = w_ref.shape[1]
        for j in range(0, n, 512):
            w = min(512, n - j)
            r = _dot(h, w_ref[:, j:j + w])
            o_ref[0, :, j:j + w] = r.astype(o_ref.dtype)
            for (tg, c0, tw), t_ref in zip(_T_COLS, t_refs):
                if tg == g and j <= c0 and c0 + tw <= j + w:
                    T = t_ref.shape[3]
                    for c in range(tm // T):
                        blk = r[c * T:(c + 1) * T, c0 - j:c0 - j + tw]
                        t_ref[0, c] = blk.T.astype(t_ref.dtype)


def _inproj(x, mod, wa, wb, wc, ws, wg, T):
    Bn, S, D = x.shape
    tm = TM_PROJ
    ws_ = [wa, wb, wc, ws, wg]
    dts = [_MXU_DTYPE, _MXU_DTYPE, _MXU_DTYPE, F32, _MXU_DTYPE]
    return pl.pallas_call(
        _inproj_kernel,
        grid=(Bn, S // tm),
        in_specs=[pl.BlockSpec((1, tm, D), lambda b, i: (b, i, 0)),
                  pl.BlockSpec((1, 3, D), lambda b, i: (b, 0, 0))]
                 + [pl.BlockSpec(w.shape, lambda b, i: (0, 0)) for w in ws_],
        out_specs=[pl.BlockSpec((1, tm, w.shape[1]), lambda b, i: (b, i, 0)) for w in ws_]
                  + [pl.BlockSpec((1, tm // T, tw, T), lambda b, i: (b, i, 0, 0))
                     for _, _, tw in _T_COLS],
        out_shape=[jax.ShapeDtypeStruct((Bn, S, w.shape[1]), dt) for w, dt in zip(ws_, dts)]
                  + [jax.ShapeDtypeStruct((Bn, S // T, tw, T), _MXU_DTYPE) for _, _, tw in _T_COLS],
        compiler_params=_cparams(("parallel", "parallel")),
        name="inproj",
    )(x, mod, *ws_)


def _compress_kernel(r_ref, rs_ref, pa_ref, pb_ref, wa_ref, wb_ref, w2_ref, o_ref):
    r = r_ref[0].astype(F32)
    rs = rs_ref[0].astype(F32)
    outs = []
    for j in range(2):
        xa = (r + pa_ref[j]).astype(_MXU_DTYPE)
        xb = (rs + pb_ref[j]).astype(_MXU_DTYPE)
        hid = jax.nn.gelu(_dot(xa, wa_ref[j]) + _dot(xb, wb_ref[j]))
        outs.append(_dot(hid.astype(_MXU_DTYPE), w2_ref[j]))
    o_ref[0] = jnp.concatenate(outs, axis=1).astype(o_ref.dtype)


def _compress(r2, r2s, pa, pb, wa, wb, w2):
    Bn, NC, KW = r2.shape
    return pl.pallas_call(
        _compress_kernel,
        grid=(Bn,),
        in_specs=[pl.BlockSpec((1, NC, KW), lambda b: (b, 0, 0)),
                  pl.BlockSpec((1, NC, KW), lambda b: (b, 0, 0)),
                  pl.BlockSpec(pa.shape, lambda b: (0, 0, 0)),
                  pl.BlockSpec(pb.shape, lambda b: (0, 0, 0)),
                  pl.BlockSpec(wa.shape, lambda b: (0, 0, 0)),
                  pl.BlockSpec(wb.shape, lambda b: (0, 0, 0)),
                  pl.BlockSpec(w2.shape, lambda b: (0, 0, 0))],
        out_specs=pl.BlockSpec((1, NC, 2 * HEAD_DIM), lambda b: (b, 0, 0)),
        out_shape=jax.ShapeDtypeStruct((Bn, NC, 2 * HEAD_DIM), _MXU_DTYPE),
        compiler_params=_cparams(("parallel",)),
        name="nsa_compress",
    )(r2, r2s, pa, pb, wa, wb, w2)


def _online_t(s, m_ref, l_ref, acc_ref, v_t):
    m = m_ref[...]
    m_new = jnp.maximum(m, jnp.max(s, axis=0, keepdims=True))
    alpha = jnp.exp2(m - m_new)
    p = jnp.exp2(s - m_new)
    l_ref[...] = alpha * l_ref[...] + jnp.sum(p, axis=0, keepdims=True)
    acc_ref[...] = alpha * acc_ref[...] + _dot(v_t, p.astype(_MXU_DTYPE))
    m_ref[...] = m_new


def _reset(m_ref, l_ref, acc_ref):
    m_ref[...] = jnp.full(m_ref.shape, NEG, F32)
    l_ref[...] = jnp.zeros(l_ref.shape, F32)
    acc_ref[...] = jnp.zeros(acc_ref.shape, F32)


def _finish_t(m_ref, l_ref, acc_ref):
    return jnp.where(m_ref[...] > 0.5 * NEG, acc_ref[...] / jnp.maximum(l_ref[...], 1e-30), 0.0)


def _pair_loop(n, step, buf_a, buf_b):
    @pl.when(n > 0)
    def _():
        step(0, buf_a, None, None)

    def pair(i, carry):
        k0 = 2 * i
        step(jnp.minimum(k0 + 1, n - 1), buf_b, k0, buf_a)

        @pl.when(k0 + 1 < n)
        def _():
            step(jnp.minimum(k0 + 2, n - 1), buf_a, k0 + 1, buf_b)
        return carry

    lax.fori_loop(0, (n + 1) // 2, pair, 0)


def _near_chunks(qi, step, buf_a, buf_b):
    @pl.when(qi >= 1)
    def _():
        step(qi - 1, buf_a, None, None)
        step(qi, buf_b, qi - 1, buf_a, 1)

    @pl.when(qi == 0)
    def _():
        step(qi, buf_b, None, None)

    step(None, None, qi, buf_b, 0)


def _zero_extended_queries(q, n_heads, width):
    t = q.shape[0]
    pad = jnp.zeros((t, LANES - width), q.dtype)
    return jnp.concatenate(
        [jnp.concatenate([q[:, h * width:(h + 1) * width], pad], axis=1) for h in range(n_heads)],
        axis=0)


def _heads_to_rows_t(o_t, n_heads):
    t = o_t.shape[1] // n_heads
    stacked = jnp.concatenate([o_t[:, h * t:(h + 1) * t] for h in range(n_heads)], axis=0)
    return stacked.T


def _dsa_kernel(aq_ref, akv_ref, vt_ref, iq_ref, ik_ref, side_ref, bias_ref, cmask_ref, o_ref,
                keys_ref, hi_ref, lo_ref, acc_ref, m_ref, l_ref, run_ref, sa_ref, sb_ref, pa_ref,
                pb_ref, *, T, topk):
    qi = pl.program_id(1)
    nk = qi + 1
    H = A_HEADS

    q_z = _zero_extended_queries(aq_ref[0], H, HEAD_DIM)
    iq_z = _zero_extended_queries(iq_ref[0], IDX_HEADS, IDX_DIM)
    iw_t = side_ref[0].T[0:IDX_HEADS] * (IDX_HEADS ** -0.5 * IDX_DIM ** -0.5)

    def chunk_rows(kc):
        return pl.ds(pl.multiple_of(kc * T, T), T)

    def raw_scores(kc):
        return _dot_nt(ik_ref[0, chunk_rows(kc), :], iq_z)

    def to_keys(kc, raw, causal_mask):
        sc = jnp.maximum(raw, 0.0)
        isc = iw_t[0:1] * sc[:, 0:T]
        for h in range(1, IDX_HEADS):
            isc = isc + iw_t[h:h + 1] * sc[:, h * T:(h + 1) * T]
        isc = isc + 0.0
        if causal_mask is not None:
            isc = isc + causal_mask
        bits = pltpu.bitcast(isc, jnp.int32)
        keys = jnp.where(bits < 0, bits ^ 0x7FFFFFFF, bits)
        keys_ref[kc] = keys
        hi_ref[kc] = lax.shift_right_arithmetic(keys, 16).astype(jnp.int16)

    def far_keys(i, carry):
        k0 = 2 * i
        k1 = jnp.minimum(k0 + 1, qi - 1)
        raw0 = raw_scores(k0)
        raw1 = raw_scores(k1)
        to_keys(k0, raw0, None)
        to_keys(k1, raw1, None)
        return carry

    lax.fori_loop(0, (qi + 1) // 2, far_keys, 0)
    to_keys(qi, raw_scores(qi), cmask_ref[...])

    n_acc = 4
    half_rows = 2 * SUBLANES

    def count16(plane_ref, cand, strict):
        cand16 = cand.astype(jnp.int16)
        one, zero = jnp.int16(1), jnp.int16(0)

        def body(kc, accs):
            k = plane_ref[kc]
            hit = jnp.where(k > cand16 if strict else k >= cand16, one, zero)
            parts = hit.reshape(T // half_rows, half_rows, T)
            accs = list(accs)
            for r in range(T // half_rows):
                accs[r % n_acc] = accs[r % n_acc] + parts[r]
            return tuple(accs)

        accs = lax.fori_loop(0, nk, body,
                             tuple(jnp.zeros((half_rows, T), jnp.int16) for _ in range(n_acc)))
        total = (accs[0] + accs[1]) + (accs[2] + accs[3])
        return jnp.sum(total.astype(F32), axis=0, keepdims=True)

    def search16(plane_ref, base_count):
        def bit_step(b, thr):
            cand = thr + lax.shift_left(jnp.int32(1), 15 - b)
            cnt = base_count + count16(plane_ref, cand, False)
            return jnp.where(cnt >= float(topk), cand, thr)
        return lax.fori_loop(0, 16, bit_step, jnp.full((1, T), -2 ** 15, jnp.int32))

    thr_hi = search16(hi_ref, 0.0)
    above_hi = count16(hi_ref, thr_hi, True)

    def low_plane(kc, carry):
        keys = keys_ref[kc]
        lo = (keys & 0xFFFF) - 2 ** 15
        same = lax.shift_right_arithmetic(keys, 16) == thr_hi
        lo_ref[kc] = jnp.where(same, lo, -2 ** 15).astype(jnp.int16)
        return carry

    lax.fori_loop(0, nk, low_plane, 0)
    thr_lo = search16(lo_ref, above_hi)
    thr = thr_hi * 2 ** 16 + (thr_lo + 2 ** 15)
    need = float(topk) - (above_hi + count16(lo_ref, thr_lo, True))

    row = lax.broadcasted_iota(jnp.int32, (T, T), 0)
    col = lax.broadcasted_iota(jnp.int32, (T, T), 1)
    lower = jnp.where(col < row, 1.0, 0.0).astype(_MXU_DTYPE)

    _reset(m_ref, l_ref, acc_ref)
    run_ref[...] = jnp.zeros(run_ref.shape, F32)

    def step(k_next, bufs_next, k_cur, bufs_cur, delta=None):
        if k_cur is not None:
            s_cur, p_cur = bufs_cur
            keys = keys_ref[k_cur]
            eq = keys == thr
            run = run_ref[...]
            sel = ((keys > thr) | (eq & (p_cur[...] + run < need))) & (keys != NEG_INF_KEY)
            run_ref[...] = run + jnp.sum(jnp.where(eq, 1.0, 0.0), axis=0, keepdims=True)
            v_t = vt_ref[0, k_cur, HEAD_DIM:2 * HEAD_DIM, :]
        if k_next is not None:
            s_next, p_next = bufs_next
            kch = akv_ref[0, chunk_rows(k_next), :]
        if k_next is not None:
            s_next[...] = _dot_nt(kch, q_z)
            eqf = jnp.where(keys_ref[k_next] == thr, 1.0, 0.0).astype(_MXU_DTYPE)
            p_next[...] = _dot(lower, eqf)
        if k_cur is not None:
            s = s_cur[...] if delta is None else s_cur[...] + bias_ref[delta]
            s = jnp.where(jnp.concatenate([sel] * H, axis=1), s, NEG)
            _online_t(s, m_ref, l_ref, acc_ref, v_t)

    _pair_loop(jnp.maximum(qi - 1, 0), step, (sa_ref, pa_ref), (sb_ref, pb_ref))
    _near_chunks(qi, step, (sa_ref, pa_ref), (sb_ref, pb_ref))

    o_ref[0] = _heads_to_rows_t(_finish_t(m_ref, l_ref, acc_ref), H).astype(o_ref.dtype)


def _dsa(apack, av_t, side, bias, cmask, T):
    Bn, S, _ = apack.shape
    topk = min(DSA_TOPK, S // 4)
    N = A_HEADS * T
    kern = functools.partial(_dsa_kernel, T=T, topk=topk)
    return pl.pallas_call(
        kern,
        grid=(Bn, S // T),
        in_specs=[pl.BlockSpec((1, T, 256), lambda b, i: (b, i, 0)),
                  pl.BlockSpec((1, S, 128), lambda b, i: (b, 0, 2)),
                  pl.BlockSpec((1,) + av_t.shape[1:], lambda b, i: (b, 0, 0, 0)),
                  pl.BlockSpec((1, T, 128), lambda b, i: (b, i, 3)),
                  pl.BlockSpec((1, S, 128), lambda b, i: (b, 0, 4)),
                  pl.BlockSpec((1, T, 128), lambda b, i: (b, i, 0)),
                  pl.BlockSpec(bias.shape, lambda b, i: (0, 0, 0)),
                  pl.BlockSpec(cmask.shape, lambda b, i: (0, 0))],
        out_specs=pl.BlockSpec((1, T, 256), lambda b, i: (b, i, 0)),
        out_shape=jax.ShapeDtypeStruct((Bn, S, 256), _MXU_DTYPE),
        scratch_shapes=[pltpu.VMEM((S // T, T, T), jnp.int32),
                        pltpu.VMEM((S // T, T, T), jnp.int16),
                        pltpu.VMEM((S // T, T, T), jnp.int16),
                        pltpu.VMEM((HEAD_DIM, N), F32),
                        pltpu.VMEM((1, N), F32), pltpu.VMEM((1, N), F32), pltpu.VMEM((1, T), F32),
                        pltpu.VMEM((T, N), F32), pltpu.VMEM((T, N), F32),
                        pltpu.VMEM((T, T), F32), pltpu.VMEM((T, T), F32)],
        compiler_params=_cparams(("parallel", "arbitrary")),
        name="dsa_attention",
    )(apack, apack, av_t, apack, apack, side, bias, cmask)


def _nsa_kernel(bq_ref, kvc_ref, vct_ref, kvs_ref, vst_ref, kvw_ref, vwt_ref, side_ref,
                bias_ref, wbias_ref, ovt_ref, o_ref, acc_ref, m_ref, l_ref, sa_ref, sb_ref,
                pa_ref, pb_ref, *, T, n_cmp, n_sel, n_win):
    qi = pl.program_id(1)
    nk = qi + 1
    H = B_HEADS
    NC = kvc_ref.shape[1]
    NS = ovt_ref.shape[0]
    log2_blk = int(math.log2(SLC_BLOCK))

    q_z = _zero_extended_queries(bq_ref[0], H, HEAD_DIM)
    t_q = qi * T + lax.broadcasted_iota(jnp.int32, (1, T), 1)

    def chunk_rows(kc):
        return pl.ds(pl.multiple_of(kc * T, T), T)

    n_idx = lax.broadcasted_iota(jnp.int32, (NC, T), 0)
    i_idx = lax.broadcasted_iota(jnp.int32, (NC, T), 1)
    cvalid = ((n_idx * CMP_STRIDE + (CMP_BLOCK - 1) - i_idx <= qi * T) & (n_idx < n_cmp))
    lc = jnp.where(jnp.concatenate([cvalid] * H, axis=1), _dot_nt(kvc_ref[0], q_z), NEG)
    mc = jnp.max(lc, axis=0, keepdims=True)
    ec = jnp.exp2(lc - mc)
    row_scale = jnp.where(mc > 0.5 * NEG,
                          1.0 / jnp.maximum(jnp.sum(ec, axis=0, keepdims=True), 1e-30), 0.0)
    pc = ec * row_scale
    o_cmp = _dot(vct_ref[0], pc.astype(_MXU_DTYPE))

    psum = pc[:, 0:T]
    for h in range(1, H):
        psum = psum + pc[:, h * T:(h + 1) * T]
    p_hi = psum.astype(_MXU_DTYPE)
    p_lo = (psum - p_hi.astype(F32)).astype(_MXU_DTYPE)
    imp = _dot(ovt_ref[...], p_hi) + _dot(ovt_ref[...], p_lo)

    blk = lax.broadcasted_iota(jnp.int32, (NS, T), 0)
    blkf = blk.astype(F32)
    cur = lax.shift_right_logical(t_q, log2_blk)
    forced = (blk == 0) | (blk == cur) | (blk == cur - 1)
    val = jnp.where(blk <= cur, jnp.where(forced, FORCE_SCORE, imp), -jnp.inf)

    def pick_one(_, carry):
        val, sel = carry
        mx = jnp.max(val, axis=0, keepdims=True)
        first = jnp.min(jnp.where(val == mx, blkf, float(NS)), axis=0, keepdims=True)
        pick = blkf == first
        sel = jnp.where(pick & (mx > -jnp.inf), 1.0, sel)
        return jnp.where(pick, -jnp.inf, val), sel

    _, sel = lax.fori_loop(0, n_sel, pick_one, (val, jnp.zeros((NS, T), F32)))
    sel = sel.astype(_MXU_DTYPE)

    e_row = lax.shift_right_logical(lax.broadcasted_iota(jnp.int32, (T, NS), 0), log2_blk)
    e_col = lax.broadcasted_iota(jnp.int32, (T, NS), 1)
    bpc = T // SLC_BLOCK

    _reset(m_ref, l_ref, acc_ref)

    def step(k_next, bufs_next, k_cur, bufs_cur, delta=None):
        if k_next is not None:
            s_next, p_next = bufs_next
            s_next[...] = _dot_nt(kvs_ref[0, chunk_rows(k_next), :], q_z)
            expand = jnp.where(e_col == k_next * bpc + e_row, 1.0, 0.0).astype(_MXU_DTYPE)
            p_next[...] = _dot(expand, sel)
        if k_cur is not None:
            s_cur, p_cur = bufs_cur
            addm = jnp.where(p_cur[...] > 0.5, 0.0, NEG)
            s = s_cur[...] + jnp.concatenate([addm] * H, axis=1)
            if delta is not None:
                s = s + bias_ref[delta]
            _online_t(s, m_ref, l_ref, acc_ref, vst_ref[0, k_cur, HEAD_DIM:2 * HEAD_DIM, :])

    _pair_loop(jnp.maximum(qi - 1, 0), step, (sa_ref, pa_ref), (sb_ref, pb_ref))
    _near_chunks(qi, step, (sa_ref, pa_ref), (sb_ref, pb_ref))
    o_slc = _finish_t(m_ref, l_ref, acc_ref)

    _reset(m_ref, l_ref, acc_ref)

    def produce_win(delta, buf):
        buf[...] = _dot_nt(kvw_ref[0, chunk_rows(jnp.maximum(qi - delta, 0)), :], q_z)

    def attend_win(delta, buf):
        kc = qi - delta
        s = buf[...] + wbias_ref[delta] + jnp.where(kc >= 0, 0.0, NEG)
        _online_t(s, m_ref, l_ref, acc_ref,
                  vwt_ref[0, jnp.maximum(kc, 0), HEAD_DIM:2 * HEAD_DIM, :])

    bufs = (sa_ref, sb_ref)
    produce_win(0, bufs[0])
    for delta in range(n_win):
        if delta + 1 < n_win:
            produce_win(delta + 1, bufs[(delta + 1) % 2])
        attend_win(delta, bufs[delta % 2])
    o_win = _finish_t(m_ref, l_ref, acc_ref)

    g = jax.nn.sigmoid(side_ref[0].T[IDX_HEADS:IDX_HEADS + 3 * H])
    outs = []
    for h in range(H):
        sl = slice(h * T, (h + 1) * T)
        outs.append(g[3 * h:3 * h + 1] * o_cmp[:, sl] + g[3 * h + 1:3 * h + 2] * o_slc[:, sl]
                    + g[3 * h + 2:3 * h + 3] * o_win[:, sl])
    o_ref[0] = jnp.concatenate(outs, axis=0).T.astype(o_ref.dtype)


def _nsa(bpack, kvc, vc_t, vs_t, vw_t, side, bias, wbias, overlap_t, T, n_cmp, n_sel):
    Bn, S, _ = bpack.shape
    NC = kvc.shape[1]
    N = B_HEADS * T
    kern = functools.partial(_nsa_kernel, T=T, n_cmp=n_cmp, n_sel=n_sel, n_win=wbias.shape[0])
    vtspec = lambda a: pl.BlockSpec((1,) + a.shape[1:], lambda b, i: (b, 0, 0, 0))
    return pl.pallas_call(
        kern,
        grid=(Bn, S // T),
        in_specs=[pl.BlockSpec((1, T, 256), lambda b, i: (b, i, 0)),
                  pl.BlockSpec((1, NC, 128), lambda b, i: (b, 0, 0)),
                  pl.BlockSpec((1, HEAD_DIM, NC), lambda b, i: (b, 0, 0)),
                  pl.BlockSpec((1, S, 128), lambda b, i: (b, 0, 3)),
                  vtspec(vs_t),
                  pl.BlockSpec((1, S, 128), lambda b, i: (b, 0, 4)),
                  vtspec(vw_t),
                  pl.BlockSpec((1, T, 128), lambda b, i: (b, i, 0)),
                  pl.BlockSpec(bias.shape, lambda b, i: (0, 0, 0)),
                  pl.BlockSpec(wbias.shape, lambda b, i: (0, 0, 0)),
                  pl.BlockSpec(overlap_t.shape, lambda b, i: (0, 0))],
        out_specs=pl.BlockSpec((1, T, 256), lambda b, i: (b, i, 0)),
        out_shape=jax.ShapeDtypeStruct((Bn, S, 256), _MXU_DTYPE),
        scratch_shapes=[pltpu.VMEM((HEAD_DIM, N), F32),
                        pltpu.VMEM((1, N), F32), pltpu.VMEM((1, N), F32),
                        pltpu.VMEM((T, N), F32), pltpu.VMEM((T, N), F32),
                        pltpu.VMEM((T, T), F32), pltpu.VMEM((T, T), F32)],
        compiler_params=_cparams(("parallel", "arbitrary")),
        name="nsa_attention",
    )(bpack, kvc, vc_t, bpack, vs_t, bpack, vw_t, side, bias, wbias, overlap_t)


def _diff_kernel(cq_ref, ck_ref, vt_ref, bias_ref, dl_ref, g_ref, o_ref, acc_ref, m_ref, l_ref,
                 sa_ref, sb_ref, *, T, lam_init):
    qi = pl.program_id(1)
    nk = qi + 1
    H = C_HEADS
    dl = dl_ref[...]
    lam = (jnp.exp(jnp.sum(dl[0:1] * dl[1:2], axis=1, keepdims=True))
           - jnp.exp(jnp.sum(dl[2:3] * dl[3:4], axis=1, keepdims=True)) + lam_init)

    cq = cq_ref[0]
    lane = lax.broadcasted_iota(jnp.int32, (T, 2 * HEAD_DIM), 1)
    q_z = []
    for h in range(H):
        qh = cq[:, h * 2 * HEAD_DIM:(h + 1) * 2 * HEAD_DIM]
        zero = jnp.zeros_like(qh)
        q_z.append(jnp.concatenate([jnp.where(lane < HEAD_DIM, qh, zero),
                                    jnp.where(lane >= HEAD_DIM, qh, zero)], axis=0))

    for h in range(H):
        _reset(m_ref.at[h], l_ref.at[h], acc_ref.at[h])

    def step(k_next, dst, k_cur, src, delta=None):
        if k_next is not None:
            kch = ck_ref[0, pl.ds(pl.multiple_of(k_next * T, T), T), :]
        if k_cur is not None:
            v_t = vt_ref[0, k_cur]
        for h in range(H):
            if k_next is not None:
                dst[h] = _dot_nt(kch[:, h * 2 * HEAD_DIM:(h + 1) * 2 * HEAD_DIM], q_z[h])
            if k_cur is not None:
                s = src[h]
                if delta is not None:
                    b = bias_ref[delta, h]
                    s = jnp.concatenate([s[:, 0:T] + b, s[:, T:2 * T] + b], axis=1)
                _online_t(s, m_ref.at[h], l_ref.at[h], acc_ref.at[h],
                          v_t[h * C_VDIM:(h + 1) * C_VDIM])

    _pair_loop(jnp.maximum(qi - 1, 0), step, sa_ref, sb_ref)
    _near_chunks(qi, step, sa_ref, sb_ref)

    outs = []
    for h in range(H):
        o = _finish_t(m_ref.at[h], l_ref.at[h], acc_ref.at[h])
        oc = o[:, 0:T] - lam * o[:, T:2 * T]
        rms = lax.rsqrt(jnp.mean(jnp.square(oc), axis=0, keepdims=True) + LN_EPS)
        outs.append(oc * rms * g_ref[...] * (1.0 - lam_init))
    o_ref[0] = jnp.concatenate(outs, axis=0).T.astype(o_ref.dtype)


def _diff(cpack, cv_t, bias, dl, g, T, lam_init):
    Bn, S, _ = cpack.shape
    W = C_HEADS * C_VDIM
    kern = functools.partial(_diff_kernel, T=T, lam_init=lam_init)
    return pl.pallas_call(
        kern,
        grid=(Bn, S // T),
        in_specs=[pl.BlockSpec((1, T, W), lambda b, i: (b, i, 0)),
                  pl.BlockSpec((1, S, W), lambda b, i: (b, 0, 1)),
                  pl.BlockSpec((1,) + cv_t.shape[1:], lambda b, i: (b, 0, 0, 0)),
                  pl.BlockSpec(bias.shape, lambda b, i: (0, 0, 0, 0)),
                  pl.BlockSpec(dl.shape, lambda b, i: (0, 0)),
                  pl.BlockSpec(g.shape, lambda b, i: (0, 0))],
        out_specs=pl.BlockSpec((1, T, W), lambda b, i: (b, i, 0)),
        out_shape=jax.ShapeDtypeStruct((Bn, S, W), _MXU_DTYPE),
        scratch_shapes=[pltpu.VMEM((C_HEADS, C_VDIM, 2 * T), F32),
                        pltpu.VMEM((C_HEADS, 1, 2 * T), F32),
                        pltpu.VMEM((C_HEADS, 1, 2 * T), F32),
                        pltpu.VMEM((C_HEADS, T, 2 * T), F32),
                        pltpu.VMEM((C_HEADS, T, 2 * T), F32)],
        compiler_params=_cparams(("parallel", "arbitrary")),
        name="diff_attention",
    )(cpack, cpack, cv_t, bias, dl, g)


def _layer_norm(v, g, b):
    mu = jnp.mean(v, axis=1, keepdims=True)
    d = v - mu
    var = jnp.mean(jnp.square(d), axis=1, keepdims=True)
    return d * lax.rsqrt(var + LN_EPS) * g + b


def _split_hi_lo(v):
    hi = v.astype(_MXU_DTYPE)
    lo = (v - hi.astype(F32)).astype(_MXU_DTYPE)
    return hi, lo


def _outproj_kernel(oa_ref, ob_ref, oc_ref, mg_ref, x_ref, mod1_ref, mod2_ref,
                    wa_ref, wb_ref, wc_ref, wo_ref, ln_ref, rw_ref, rb_ref,
                    x1_ref, h2_ref, tr_ref, tw_ref, cnt_ref, run_ref, *, dn_alpha, n_experts):
    D = x_ref.shape[2]
    tm = x_ref.shape[1]
    first_step = (pl.program_id(0) == 0) & (pl.program_id(1) == 0)

    @pl.when(first_step)
    def _():
        run_ref[...] = jnp.zeros(run_ref.shape, F32)

    ya = _dot(oa_ref[0], wa_ref[...])
    yb = _dot(ob_ref[0], wb_ref[...])
    yc = _dot(oc_ref[0], wc_ref[...])
    mg = mg_ref[0].astype(F32)
    merged = (jax.nn.sigmoid(mg[:, 0:D]) * ya + jax.nn.sigmoid(mg[:, D:2 * D]) * yb
              + jax.nn.sigmoid(mg[:, 2 * D:3 * D]) * yc)
    y = _dot(merged.astype(_MXU_DTYPE), wo_ref[...])
    gate1 = mod1_ref[0, 2:3, :]
    x1 = _layer_norm(dn_alpha * x_ref[0] + (1.0 + gate1) * y, ln_ref[0:1, :], ln_ref[1:2, :])
    x1_ref[0] = x1
    h2 = x1 * (1.0 + mod2_ref[0, 1:2, :]) + mod2_ref[0, 0:1, :]
    h2_ref[0] = h2.astype(h2_ref.dtype)

    h_hi, h_lo = _split_hi_lo(h2)
    w_hi, w_lo = _split_hi_lo(rw_ref[...])
    hw = _dot(h_hi, jnp.concatenate([w_hi, w_lo], axis=1))
    logits = hw[:, 0:LANES] + hw[:, LANES:2 * LANES] + _dot(h_lo, w_hi) + rb_ref[...]
    lane = lax.broadcasted_iota(jnp.int32, logits.shape, 1)
    lanef = lane.astype(F32)
    work = jnp.where(lane < n_experts, logits, -jnp.inf)
    tv = jnp.zeros(logits.shape, F32)
    firsts, onehots = [], []
    v0 = None
    for k in range(TOP_K):
        mx = jnp.max(work, axis=1, keepdims=True)
        first = jnp.min(jnp.where(work == mx, lanef, float(LANES)), axis=1, keepdims=True)
        if k == 0:
            v0 = mx
        tv = jnp.where(lane == k, jnp.exp(mx - v0), tv)
        hit = lanef == first
        work = jnp.where(hit, -jnp.inf, work)
        firsts.append(first)
        onehots.append(jnp.where(hit, 1.0, 0.0))
    tw_ref[0] = tv / jnp.sum(tv, axis=1, keepdims=True)

    chosen = onehots[0] + onehots[1] + onehots[2] + onehots[3]
    r_i = lax.broadcasted_iota(jnp.int32, (tm, tm), 0)
    c_i = lax.broadcasted_iota(jnp.int32, (tm, tm), 1)
    earlier = jnp.where(c_i < r_i, 1.0, 0.0).astype(_MXU_DTYPE)
    base = _dot(earlier, chosen.astype(_MXU_DTYPE)) + run_ref[0:1, :]
    tr = jnp.zeros(logits.shape, F32)
    for k in range(TOP_K):
        rank = jnp.sum(onehots[k] * base, axis=1, keepdims=True)
        tr = jnp.where(lane == k, firsts[k], tr)
        tr = jnp.where(lane == TOP_K + k, rank, tr)
    tr_ref[0] = tr.astype(jnp.int32)
    run_ref[...] = run_ref[...] + jnp.sum(chosen, axis=0, keepdims=True)
    cnt_ref[...] = run_ref[...]


def _outproj(oa, ob, oc, mg, x, mod1, mod2, wa, wb, wc, wo, ln, rw, rb, dn_alpha, n_experts):
    Bn, S, D = x.shape
    tm = TM_PROJ
    row = lambda w: pl.BlockSpec((1, tm, w), lambda b, i: (b, i, 0))
    full = lambda a: pl.BlockSpec(a.shape, lambda b, i: (0,) * a.ndim)
    modspec = pl.BlockSpec((1, 3, D), lambda b, i: (b, 0, 0))
    kern = functools.partial(_outproj_kernel, dn_alpha=dn_alpha, n_experts=n_experts)
    return pl.pallas_call(
        kern,
        grid=(Bn, S // tm),
        in_specs=[row(oa.shape[2]), row(ob.shape[2]), row(oc.shape[2]), row(mg.shape[2]), row(D),
                  modspec, modspec, full(wa), full(wb), full(wc), full(wo), full(ln), full(rw),
                  full(rb)],
        out_specs=[row(D), row(D), row(LANES), row(LANES),
                   pl.BlockSpec((SUBLANES, LANES), lambda b, i: (0, 0))],
        out_shape=[jax.ShapeDtypeStruct((Bn, S, D), F32),
                   jax.ShapeDtypeStruct((Bn, S, D), _MXU_DTYPE),
                   jax.ShapeDtypeStruct((Bn, S, LANES), jnp.int32),
                   jax.ShapeDtypeStruct((Bn, S, LANES), F32),
                   jax.ShapeDtypeStruct((SUBLANES, LANES), F32)],
        scratch_shapes=[pltpu.VMEM((SUBLANES, LANES), F32)],
        compiler_params=_cparams(("arbitrary", "arbitrary")),
        name="merge_outproj_ln_router",
    )(oa, ob, oc, mg, x, mod1, mod2, wa, wb, wc, wo, ln, rw, rb)


def _deinterleave_kernel(w_ref, p_ref, og_ref, ou_ref):
    n = w_ref.shape[1]
    for j in range(n // (2 * LANES)):
        blk = w_ref[:, j * 2 * LANES:(j + 1) * 2 * LANES].astype(_MXU_DTYPE)
        r = _dot(blk, p_ref[...])
        og_ref[:, j * LANES:(j + 1) * LANES] = r[:, 0:LANES].astype(og_ref.dtype)
        ou_ref[:, j * LANES:(j + 1) * LANES] = r[:, LANES:2 * LANES].astype(ou_ref.dtype)


def _deinterleave(w_all, part, n_parts):
    N2 = w_all.shape[1]
    R = w_all.shape[0] // n_parts
    tr = 1024
    first = part * (R // tr)
    perm = np.zeros((2 * LANES, 2 * LANES), np.float32)
    perm[2 * np.arange(LANES), np.arange(LANES)] = 1.0
    perm[2 * np.arange(LANES) + 1, LANES + np.arange(LANES)] = 1.0
    return pl.pallas_call(
        _deinterleave_kernel,
        grid=(R // tr,),
        in_specs=[pl.BlockSpec((tr, N2), lambda i: (first + i, 0)),
                  pl.BlockSpec(perm.shape, lambda i: (0, 0))],
        out_specs=[pl.BlockSpec((tr, N2 // 2), lambda i: (i, 0))] * 2,
        out_shape=[jax.ShapeDtypeStruct((R, N2 // 2), _MXU_DTYPE)] * 2,
        compiler_params=_cparams(("parallel",)),
        name="expert_weight_deinterleave",
    )(w_all, jnp.asarray(perm).astype(_MXU_DTYPE))


def _moe_kernel(te_ref, tv_ref, x_ref, wg_ref, wu_ref, wd_ref, bg_ref, bu_ref, bd_ref, o_ref):
    i = pl.program_id(0)

    @pl.when(tv_ref[i] > 0)
    def _():
        x = x_ref[...]
        dff = wg_ref.shape[2]
        ck = 512
        y = jnp.zeros(o_ref.shape, F32)
        for j in range(0, dff, ck):
            gate = jnp.minimum(_dot(x, wg_ref[0, :, j:j + ck]) + bg_ref[0, :, j:j + ck], SWIGLU_LIMIT)
            up = jnp.clip(_dot(x, wu_ref[0, :, j:j + ck]) + bu_ref[0, :, j:j + ck],
                          -SWIGLU_LIMIT, SWIGLU_LIMIT)
            act = (up + 1.0) * (gate * jax.nn.sigmoid(SWIGLU_ALPHA * gate))
            y = y + _dot(act.astype(_MXU_DTYPE), wd_ref[0, j:j + ck, :])
        o_ref[...] = (y + bd_ref[0]).astype(o_ref.dtype)


def _moe(xs, tile_expert, tile_valid, wg, wu, wd_all, wd_first, bg, bu, bd):
    R, D = xs.shape
    tm = TM_MOE
    nt = R // tm
    wspec = lambda a: pl.BlockSpec((1,) + a.shape[1:], lambda i, te, tv: (te[i], 0, 0))
    wdspec = pl.BlockSpec((1,) + wd_all.shape[1:], lambda i, te, tv: (wd_first + te[i], 0, 0))
    grid_spec = pltpu.PrefetchScalarGridSpec(
        num_scalar_prefetch=2,
        grid=(nt,),
        in_specs=[pl.BlockSpec((tm, D), lambda i, te, tv: (i, 0)),
                  wspec(wg), wspec(wu), wdspec, wspec(bg), wspec(bu), wspec(bd)],
        out_specs=pl.BlockSpec((tm, D), lambda i, te, tv: (i, 0)),
    )
    return pl.pallas_call(
        _moe_kernel,
        grid_spec=grid_spec,
        out_shape=jax.ShapeDtypeStruct((R, D), _MXU_DTYPE),
        compiler_params=_cparams(("arbitrary",)),
        name="moe_grouped_mlp",
    )(tile_expert, tile_valid, xs, wg, wu, wd_all, bg, bu, bd)


def _combine_kernel(yg_ref, tw_ref, x_ref, mod_ref, ln_ref, o_ref, *, dn_alpha):
    tw = tw_ref[0]
    y = tw[:, 0:1] * yg_ref[0, 0].astype(F32)
    for k in range(1, TOP_K):
        y = y + tw[:, k:k + 1] * yg_ref[k, 0].astype(F32)
    gate = mod_ref[0, 2:3, :]
    o_ref[0] = _layer_norm(dn_alpha * x_ref[0] + (1.0 + gate) * y, ln_ref[0:1, :], ln_ref[1:2, :])


def _combine(yg, tw, x, mod, ln, dn_alpha):
    Bn, S, D = x.shape
    tm = TM_PROJ
    row = lambda w: pl.BlockSpec((1, tm, w), lambda b, i: (b, i, 0))
    return pl.pallas_call(
        functools.partial(_combine_kernel, dn_alpha=dn_alpha),
        grid=(Bn, S // tm),
        in_specs=[pl.BlockSpec((TOP_K, 1, tm, D), lambda b, i: (0, b, i, 0)), row(LANES), row(D),
                  pl.BlockSpec((1, 3, D), lambda b, i: (b, 0, 0)),
                  pl.BlockSpec(ln.shape, lambda b, i: (0, 0))],
        out_specs=row(D),
        out_shape=jax.ShapeDtypeStruct((Bn, S, D), F32),
        compiler_params=_cparams(("parallel", "parallel")),
        name="moe_combine_ln",
    )(yg, tw, x, mod, ln)


def _bucket_np(dist):
    exact = N_BUCKETS // 2
    n = np.maximum(dist, 0)
    nf = np.maximum(n, 1).astype(np.float32)
    large = exact + (np.log(nf / exact) / math.log(MAX_DISTANCE / exact)
                     * (N_BUCKETS - exact)).astype(np.int32)
    large = np.minimum(large, N_BUCKETS - 1)
    return np.where(n < exact, n, large)


def _bias_tiles_t(tab, T, n_delta, window=None, shift_far=True):
    H = tab.shape[1]
    L = 2 * T
    k = np.arange(L)
    diff = np.where(k < T, k, k - L)
    d = diff[None, :] + (np.arange(n_delta) * T)[:, None]
    ok = d >= 0 if window is None else (d >= 0) & (d < window)
    onehot = np.eye(N_BUCKETS, dtype=np.float32)[_bucket_np(d)]
    prof = jnp.einsum("dlb,bh->dhl", jnp.asarray(onehot), tab, precision=lax.Precision.HIGHEST)
    if shift_far:
        prof = prof - tab[N_BUCKETS - 1][None, :, None]
    prof = jnp.where(jnp.asarray(ok)[:, None, :], prof * LOG2E, NEG)
    rolled = jnp.tile(prof, (1, 1, T))[:, :, :T * (L - 1)].reshape(n_delta, H, T, L - 1)
    return rolled[:, :, :, :T]


def _stack_tiles(tiles):
    n, H, T, _ = tiles.shape
    return jnp.transpose(tiles, (0, 2, 1, 3)).reshape(n, T, H * T)


def _slc_overlap_np(n_cmp, n_slc, rows):
    start = np.arange(n_cmp) * CMP_STRIDE
    end = start + CMP_BLOCK
    bs = np.arange(n_slc) * SLC_BLOCK
    ov = (start[:, None] < bs[None, :] + SLC_BLOCK) & (end[:, None] > bs[None, :])
    out = np.zeros((rows, n_slc), np.float32)
    out[:n_cmp] = ov
    return out


def _routing(top_i, rank, counts, tm):
    N = top_i.shape[0]
    E = counts.shape[0]
    A = N * TOP_K
    nt = A // tm + E
    padded = ((counts + tm - 1) // tm) * tm
    starts = jnp.cumsum(counts) - counts
    pstarts = jnp.cumsum(padded) - padded
    pend = pstarts + padded
    tile_start = jnp.arange(nt, dtype=jnp.int32) * tm
    tile_valid = (tile_start < pend[-1]).astype(jnp.int32)
    te = jnp.minimum(jnp.sum((tile_start[:, None] >= pend[None, :]).astype(jnp.int32), axis=1), E - 1)
    last_valid = jnp.max(jnp.where(tile_valid > 0, te, 0))
    tile_expert = jnp.where(tile_valid > 0, te, last_valid).astype(jnp.int32)
    onehot_t = tile_expert[:, None] == jnp.arange(E)[None, :]
    pick = lambda tab: jnp.sum(jnp.where(onehot_t, tab[None, :], 0), axis=1)
    order = jnp.argsort(top_i.reshape(A), stable=True).astype(jnp.int32)
    in_group = (tile_start - pick(pstarts))[:, None] + jnp.arange(tm, dtype=jnp.int32)[None, :]
    ok = (in_group < pick(counts)[:, None]) & (tile_valid[:, None] > 0)
    slot = jnp.clip(pick(starts)[:, None] + in_group, 0, A - 1)
    filler = (jnp.arange(nt * tm, dtype=jnp.int32) % N).reshape(nt, tm)
    row_tok = jnp.where(ok, jnp.take(order, slot.reshape(-1), mode="clip").reshape(nt, tm) // TOP_K,
                        filler)
    onehot_a = top_i[:, :, None] == jnp.arange(E)[None, None, :]
    pos = rank + jnp.sum(jnp.where(onehot_a, pstarts[None, None, :], 0), axis=2)
    return tile_expert, tile_valid, row_tok.reshape(nt * tm), pos.astype(jnp.int32)


def kernel(x, c, rel_bias, mod_attn_w, mod_attn_b, w_in, cmp_pos, cmp_w1, cmp_w2, diff_lambda,
           diff_norm_g, w_branch_a, w_branch_b, w_branch_c, w_out, ln1_g, ln1_b, mod_ffn_w,
           mod_ffn_b, router_w, router_b, exp_w_gu, exp_b_gu, exp_w_down, exp_b_down, ln2_g, ln2_b):
    Bn, S, D = x.shape
    L = w_in.shape[0]
    E = exp_w_gu.shape[1]
    T = T_ATT
    cdt = _MXU_DTYPE
    dn_alpha = (2 * L) ** 0.25
    assert S % T == 0 and S % TM_PROJ == 0 and (Bn * S * TOP_K) % TM_MOE == 0
    assert (S // SLC_BLOCK) % SUBLANES == 0 and E <= LANES and T >= MAX_DISTANCE

    mod_w = jnp.stack([mod_attn_w, mod_ffn_w], axis=1).reshape(2 * L, D, 3 * D)
    mod_b = jnp.stack([mod_attn_b, mod_ffn_b], axis=1).reshape(2 * L, 3 * D)
    mods = _adaln(c, mod_w, mod_b).reshape(2 * L, Bn, 3, D)

    n_win = WINDOW // T + 1
    tab_b = rel_bias[:, A_HEADS:A_HEADS + B_HEADS]
    bias_a = _stack_tiles(_bias_tiles_t(rel_bias[:, :A_HEADS], T, 2))
    bias_b = _stack_tiles(_bias_tiles_t(tab_b, T, 2))
    bias_w = _stack_tiles(_bias_tiles_t(tab_b, T, n_win, WINDOW, shift_far=False))
    bias_c = _bias_tiles_t(rel_bias[:, A_HEADS + B_HEADS:], T, 2)
    jj = np.arange(T)
    cmask = jnp.asarray(np.where(jj[None, :] >= jj[:, None], 0.0, -np.inf).astype(np.float32))

    n_cmp = (S - CMP_BLOCK) // CMP_STRIDE + 1
    n_slc = S // SLC_BLOCK
    n_sel = min(SLC_TOPN, n_slc)
    NC = S // CMP_STRIDE
    overlap_t = jnp.asarray(_slc_overlap_np(n_cmp, n_slc, NC).T).astype(cdt)

    rows16 = CMP_STRIDE * 2 * HEAD_DIM
    w_gu_all = exp_w_gu.reshape(L * E * D, -1)
    w_down_all = exp_w_down.astype(cdt).reshape(L * E, exp_w_down.shape[2], D)

    for l in range(L):
        lam_init = 0.8 - 0.6 * math.exp(-0.3 * l)
        wl = w_in[l]
        wa = jnp.pad(jnp.concatenate([wl[:, 0:256] * Q_FOLD, wl[:, 256:544]], axis=1),
                     ((0, 0), (0, 96))).astype(cdt)
        wb = jnp.concatenate([wl[:, 548:804] * Q_FOLD, wl[:, 804:1188]], axis=1).astype(cdt)
        wc = jnp.concatenate([wl[:, 1200:1712] * Q_FOLD, wl[:, 1712:2736]], axis=1).astype(cdt)
        ws = jnp.pad(jnp.concatenate([wl[:, 544:548], wl[:, 1188:1200]], axis=1),
                     ((0, 0), (0, LANES - 16))).astype(cdt)
        wg = wl[:, 2736:].astype(cdt)

        apack, bpack, cpack, side, mg, akv_t, kvs_t, kvw_t, cv_t = _inproj(
            x, mods[2 * l], wa, wb, wc, ws, wg, T)

        r2 = bpack[:, :, 256:384].reshape(Bn, NC, rows16)
        r2s = jnp.concatenate([r2[:, 1:], jnp.zeros_like(r2[:, :1])], axis=1)
        half = CMP_BLOCK // 2
        zpad = jnp.zeros((half, HEAD_DIM), F32)
        pa, pb, wca, wcb = [], [], [], []
        for j in range(2):
            pos = cmp_pos[l, j]
            w1 = cmp_w1[l, j].reshape(CMP_BLOCK, HEAD_DIM, CMP_HIDDEN)
            zw = jnp.zeros((half, HEAD_DIM, CMP_HIDDEN), F32)
            kv = (lambda a, z: jnp.concatenate([a, z], axis=1)) if j == 0 else \
                 (lambda a, z: jnp.concatenate([z, a], axis=1))
            pa.append(kv(pos[:half], zpad).reshape(1, rows16))
            pb.append(kv(pos[half:], zpad).reshape(1, rows16))
            wca.append(kv(w1[:half], zw).reshape(rows16, CMP_HIDDEN))
            wcb.append(kv(w1[half:], zw).reshape(rows16, CMP_HIDDEN))
        kvc = _compress(r2, r2s, jnp.stack(pa), jnp.stack(pb), jnp.stack(wca).astype(cdt),
                        jnp.stack(wcb).astype(cdt), cmp_w2[l].astype(cdt))

        oa = _dsa(apack, akv_t, side, bias_a, cmask, T)
        ob = _nsa(bpack, kvc, jnp.transpose(kvc[:, :, HEAD_DIM:], (0, 2, 1)), kvs_t, kvw_t,
                  side, bias_b, bias_w, overlap_t, T, n_cmp, n_sel)
        oc = _diff(cpack, cv_t, bias_c, diff_lambda[l], diff_norm_g[l].reshape(C_VDIM, 1), T,
                   lam_init)

        rw = jnp.pad(router_w[l], ((0, 0), (0, LANES - E)))
        rb = jnp.pad(router_b[l], (0, LANES - E)).reshape(1, LANES)
        x1, h2, tr, tw, cnt = _outproj(
            oa, ob, oc, mg, x, mods[2 * l], mods[2 * l + 1],
            w_branch_a[l].astype(cdt), w_branch_b[l].astype(cdt), w_branch_c[l].astype(cdt),
            w_out[l].astype(cdt), jnp.stack([ln1_g[l], ln1_b[l]]), rw, rb, dn_alpha, E)

        N = Bn * S
        tr = tr.reshape(N, LANES)
        tile_expert, tile_valid, row_tok, pos = _routing(
            tr[:, 0:TOP_K], tr[:, TOP_K:2 * TOP_K], cnt[0, :E].astype(jnp.int32), TM_MOE)
        xs = jnp.take(h2.reshape(N, D), row_tok, axis=0, mode="clip")
        wgate, wup = _deinterleave(w_gu_all, l, L)
        bgu = exp_b_gu[l].reshape(E, 1, -1, 2)
        ys = _moe(xs, tile_expert, tile_valid,
                  wgate.reshape(E, D, -1), wup.reshape(E, D, -1), w_down_all, l * E,
                  bgu[..., 0], bgu[..., 1], exp_b_down[l].reshape(E, 1, D))
        yg = jnp.take(ys, pos.T.reshape(-1), axis=0, mode="clip").reshape(TOP_K, Bn, S, D)
        x = _combine(yg, tw, x1, mods[2 * l + 1], jnp.stack([ln2_g[l], ln2_b[l]]), dn_alpha)
    return x
```

```python
import functools
import math

import numpy as np
import jax
import jax.numpy as jnp
from jax import lax
from jax.experimental import pallas as pl
from jax.experimental.pallas import tpu as pltpu

F32 = jnp.float32
_MXU_DTYPE = jnp.bfloat16

HEAD_DIM = 64
A_HEADS = 4
IDX_HEADS = 4
IDX_DIM = 32
DSA_TOPK = 256
B_HEADS = 4
CMP_BLOCK = 32
CMP_STRIDE = 16
CMP_HIDDEN = 256
SLC_BLOCK = 64
SLC_TOPN = 16
WINDOW = 512
FORCE_SCORE = 1e9
C_HEADS = 4
C_VDIM = 2 * HEAD_DIM
N_BUCKETS = 32
MAX_DISTANCE = 128
TOP_K = 4
SWIGLU_LIMIT = 7.0
SWIGLU_ALPHA = 1.702
LN_EPS = 1e-5

LOG2E = math.log2(math.e)
Q_FOLD = HEAD_DIM ** -0.5 * LOG2E
NEG = -1e30
INT_MIN = -2 ** 31
NEG_INF_KEY = INT_MIN + 0x7FFFFF

LANES = 128
SUBLANES = 8
VMEM_LIMIT_BYTES = 56 * 1024 * 1024

T_ATT = 256
TM_PROJ = 512
TM_MOE = 512


def _cparams(sem):
    return pltpu.CompilerParams(dimension_semantics=sem, vmem_limit_bytes=VMEM_LIMIT_BYTES)


def _dot(a, b):
    return jnp.dot(a, b, preferred_element_type=F32)


def _dot_nt(a, b):
    return lax.dot_general(a, b, (((1,), (1,)), ((), ())), preferred_element_type=F32)


def _adaln_kernel(c_ref, w_ref, b_ref, o_ref):
    c = c_ref[...]
    a = (c * jax.nn.sigmoid(c)).astype(_MXU_DTYPE)
    o_ref[0] = _dot(a, w_ref[0].astype(_MXU_DTYPE)) + b_ref[0]


def _adaln(c, w, b):
    G, D, N = w.shape
    Bn = c.shape[0]
    tn = 768
    return pl.pallas_call(
        _adaln_kernel,
        grid=(G, N // tn),
        in_specs=[
            pl.BlockSpec((Bn, D), lambda g, j: (0, 0)),
            pl.BlockSpec((1, D, tn), lambda g, j: (g, 0, j)),
            pl.BlockSpec((1, 1, tn), lambda g, j: (g, 0, j)),
        ],
        out_specs=pl.BlockSpec((1, Bn, tn), lambda g, j: (g, 0, j)),
        out_shape=jax.ShapeDtypeStruct((G, Bn, N), F32),
        compiler_params=_cparams(("parallel", "parallel")),
        name="adaln",
    )(c, w, b.reshape(G, 1, N))


_T_COLS = ((0, 256, 128), (1, 384, 128), (1, 512, 128), (2, 1024, 512))


def _inproj_kernel(x_ref, mod_ref, wa_ref, wb_ref, wc_ref, ws_ref, wg_ref,
                   oa_ref, ob_ref, oc_ref, os_ref, og_ref, *t_refs):
    x = x_ref[0]
    tm = x.shape[0]
    shift = mod_ref[0, 0:1, :]
    scale = mod_ref[0, 1:2, :]
    h = (x * (1.0 + scale) + shift).astype(_MXU_DTYPE)
    for g, (w_ref, o_ref) in enumerate(((wa_ref, oa_ref), (wb_ref, ob_ref), (wc_ref, oc_ref),
                                        (ws_ref, os_ref), (wg_ref, og_ref))):
        n = w_ref.shape[1]
        for j in range(0, n, 512):
            w = min(512, n - j)
            r = _dot(h, w_ref[:, j:j + w])
            o_ref[0, :, j:j + w] = r.astype(o_ref.dtype)
            for (tg, c0, tw), t_ref in zip(_T_COLS, t_refs):
                if tg == g and j <= c0 and c0 + tw <= j + w:
                    T = t_ref.shape[3]
                    for c in range(tm // T):
                        blk = r[c * T:(c + 1) * T, c0 - j:c0 - j + tw]
                        t_ref[0, c] = blk.T.astype(t_ref.dtype)


def _inproj(x, mod, wa, wb, wc, ws, wg, T):
    Bn, S, D = x.shape
    tm = TM_PROJ
    ws_ = [wa, wb, wc, ws, wg]
    dts = [_MXU_DTYPE, _MXU_DTYPE, _MXU_DTYPE, F32, _MXU_DTYPE]
    return pl.pallas_call(
        _inproj_kernel,
        grid=(Bn, S // tm),
        in_specs=[pl.BlockSpec((1, tm, D), lambda b, i: (b, i, 0)),
                  pl.BlockSpec((1, 3, D), lambda b, i: (b, 0, 0))]
                 + [pl.BlockSpec(w.shape, lambda b, i: (0, 0)) for w in ws_],
        out_specs=[pl.BlockSpec((1, tm, w.shape[1]), lambda b, i: (b, i, 0)) for w in ws_]
                  + [pl.BlockSpec((1, tm // T, tw, T), lambda b, i: (b, i, 0, 0))
                     for _, _, tw in _T_COLS],
        out_shape=[jax.ShapeDtypeStruct((Bn, S, w.shape[1]), dt) for w, dt in zip(ws_, dts)]
                  + [jax.ShapeDtypeStruct((Bn, S // T, tw, T), _MXU_DTYPE) for _, _, tw in _T_COLS],
        compiler_params=_cparams(("parallel", "parallel")),
        name="inproj",
    )(x, mod, *ws_)


def _compress_kernel(r_ref, rs_ref, pa_ref, pb_ref, wa_ref, wb_ref, w2_ref, o_ref):
    r = r_ref[0].astype(F32)
    rs = rs_ref[0].astype(F32)
    outs = []
    for j in range(2):
        xa = (r + pa_ref[j]).astype(_MXU_DTYPE)
        xb = (rs + pb_ref[j]).astype(_MXU_DTYPE)
        hid = jax.nn.gelu(_dot(xa, wa_ref[j]) + _dot(xb, wb_ref[j]))
        outs.append(_dot(hid.astype(_MXU_DTYPE), w2_ref[j]))
    o_ref[0] = jnp.concatenate(outs, axis=1).astype(o_ref.dtype)


def _compress(r2, r2s, pa, pb, wa, wb, w2):
    Bn, NC, KW = r2.shape
    return pl.pallas_call(
        _compress_kernel,
        grid=(Bn,),
        in_specs=[pl.BlockSpec((1, NC, KW), lambda b: (b, 0, 0)),
                  pl.BlockSpec((1, NC, KW), lambda b: (b, 0, 0)),
                  pl.BlockSpec(pa.shape, lambda b: (0, 0, 0)),
                  pl.BlockSpec(pb.shape, lambda b: (0, 0, 0)),
                  pl.BlockSpec(wa.shape, lambda b: (0, 0, 0)),
                  pl.BlockSpec(wb.shape, lambda b: (0, 0, 0)),
                  pl.BlockSpec(w2.shape, lambda b: (0, 0, 0))],
        out_specs=pl.BlockSpec((1, NC, 2 * HEAD_DIM), lambda b: (b, 0, 0)),
        out_shape=jax.ShapeDtypeStruct((Bn, NC, 2 * HEAD_DIM), _MXU_DTYPE),
        compiler_params=_cparams(("parallel",)),
        name="nsa_compress",
    )(r2, r2s, pa, pb, wa, wb, w2)


def _online_t(s, m_ref, l_ref, acc_ref, v_t):
    m = m_ref[...]
    m_new = jnp.maximum(m, jnp.max(s, axis=0, keepdims=True))
    alpha = jnp.exp2(m - m_new)
    p = jnp.exp2(s - m_new)
    l_ref[...] = alpha * l_ref[...] + jnp.sum(p, axis=0, keepdims=True)
    acc_ref[...] = alpha * acc_ref[...] + _dot(v_t, p.astype(_MXU_DTYPE))
    m_ref[...] = m_new


def _reset(m_ref, l_ref, acc_ref):
    m_ref[...] = jnp.full(m_ref.shape, NEG, F32)
    l_ref[...] = jnp.zeros(l_ref.shape, F32)
    acc_ref[...] = jnp.zeros(acc_ref.shape, F32)


def _finish_t(m_ref, l_ref, acc_ref):
    return jnp.where(m_ref[...] > 0.5 * NEG, acc_ref[...] / jnp.maximum(l_ref[...], 1e-30), 0.0)


def _pair_loop(n, step, buf_a, buf_b):
    @pl.when(n > 0)
    def _():
        step(0, buf_a, None, None)

    def pair(i, carry):
        k0 = 2 * i
        step(jnp.minimum(k0 + 1, n - 1), buf_b, k0, buf_a)

        @pl.when(k0 + 1 < n)
        def _():
            step(jnp.minimum(k0 + 2, n - 1), buf_a, k0 + 1, buf_b)
        return carry

    lax.fori_loop(0, (n + 1) // 2, pair, 0)


def _near_chunks(qi, step, buf_a, buf_b):
    @pl.when(qi >= 1)
    def _():
        step(qi - 1, buf_a, None, None)
        step(qi, buf_b, qi - 1, buf_a, 1)

    @pl.when(qi == 0)
    def _():
        step(qi, buf_b, None, None)

    step(None, None, qi, buf_b, 0)


def _zero_extended_queries(q, n_heads, width):
    t = q.shape[0]
    pad = jnp.zeros((t, LANES - width), q.dtype)
    return jnp.concatenate(
        [jnp.concatenate([q[:, h * width:(h + 1) * width], pad], axis=1) for h in range(n_heads)],
        axis=0)


def _heads_to_rows_t(o_t, n_heads):
    t = o_t.shape[1] // n_heads
    stacked = jnp.concatenate([o_t[:, h * t:(h + 1) * t] for h in range(n_heads)], axis=0)
    return stacked.T


def _dsa_kernel(aq_ref, akv_ref, vt_ref, iq_ref, ik_ref, side_ref, bias_ref, cmask_ref, o_ref,
                keys_ref, hi_ref, lo_ref, acc_ref, m_ref, l_ref, run_ref, sa_ref, sb_ref, pa_ref,
                pb_ref, *, T, topk):
    qi = pl.program_id(1)
    nk = qi + 1
    H = A_HEADS

    q_z = _zero_extended_queries(aq_ref[0], H, HEAD_DIM)
    iq_z = _zero_extended_queries(iq_ref[0], IDX_HEADS, IDX_DIM)
    iw_t = side_ref[0].T[0:IDX_HEADS] * (IDX_HEADS ** -0.5 * IDX_DIM ** -0.5)

    def chunk_rows(kc):
        return pl.ds(pl.multiple_of(kc * T, T), T)

    def raw_scores(kc):
        return _dot_nt(ik_ref[0, chunk_rows(kc), :], iq_z)

    def to_keys(kc, raw, causal_mask):
        sc = jnp.maximum(raw, 0.0)
        isc = iw_t[0:1] * sc[:, 0:T]
        for h in range(1, IDX_HEADS):
            isc = isc + iw_t[h:h + 1] * sc[:, h * T:(h + 1) * T]
        isc = isc + 0.0
        if causal_mask is not None:
            isc = isc + causal_mask
        bits = pltpu.bitcast(isc, jnp.int32)
        keys = jnp.where(bits < 0, bits ^ 0x7FFFFFFF, bits)
        keys_ref[kc] = keys
        hi_ref[kc] = lax.shift_right_arithmetic(keys, 16).astype(jnp.int16)

    def far_keys(i, carry):
        k0 = 2 * i
        k1 = jnp.minimum(k0 + 1, qi - 1)
        raw0 = raw_scores(k0)
        raw1 = raw_scores(k1)
        to_keys(k0, raw0, None)
        to_keys(k1, raw1, None)
        return carry

    lax.fori_loop(0, (qi + 1) // 2, far_keys, 0)
    to_keys(qi, raw_scores(qi), cmask_ref[...])

    n_acc = 4
    half_rows = 2 * SUBLANES

    def count16(plane_ref, cand):
        cand16 = cand.astype(jnp.int16)
        one, zero = jnp.int16(1), jnp.int16(0)

        def body(kc, accs):
            k = plane_ref[kc]
            hit = jnp.where(k >= cand16, one, zero)
            parts = hit.reshape(T // half_rows, half_rows, T)
            accs = list(accs)
            for r in range(T // half_rows):
                accs[r % n_acc] = accs[r % n_acc] + parts[r]
            return tuple(accs)

        accs = lax.fori_loop(0, nk, body,
                             tuple(jnp.zeros((half_rows, T), jnp.int16) for _ in range(n_acc)))
        total = (accs[0] + accs[1]) + (accs[2] + accs[3])
        return jnp.sum(total.astype(F32), axis=0, keepdims=True)

    def search16(plane_ref, base_count):
        def bit_step(b, carry):
            thr, above = carry
            cand = thr + lax.shift_left(jnp.int32(1), 15 - b)
            cnt = base_count + count16(plane_ref, cand)
            take = cnt >= float(topk)
            return jnp.where(take, cand, thr), jnp.where(take, above, cnt)
        init = (jnp.full((1, T), -2 ** 15, jnp.int32), jnp.zeros((1, T), F32) + base_count)
        return lax.fori_loop(0, 16, bit_step, init)

    thr_hi, above_hi = search16(hi_ref, 0.0)

    def low_plane(kc, carry):
        keys = keys_ref[kc]
        lo = (keys & 0xFFFF) - 2 ** 15
        same = lax.shift_right_arithmetic(keys, 16) == thr_hi
        lo_ref[kc] = jnp.where(same, lo, -2 ** 15).astype(jnp.int16)
        return carry

    lax.fori_loop(0, nk, low_plane, 0)
    thr_lo, above = search16(lo_ref, above_hi)
    thr = thr_hi * 2 ** 16 + (thr_lo + 2 ** 15)
    need = float(topk) - above

    row = lax.broadcasted_iota(jnp.int32, (T, T), 0)
    col = lax.broadcasted_iota(jnp.int32, (T, T), 1)
    lower = jnp.where(col < row, 1.0, 0.0).astype(_MXU_DTYPE)

    _reset(m_ref, l_ref, acc_ref)
    run_ref[...] = jnp.zeros(run_ref.shape, F32)

    def step(k_next, bufs_next, k_cur, bufs_cur, delta=None):
        if k_cur is not None:
            s_cur, p_cur = bufs_cur
            keys = keys_ref[k_cur]
            eq = keys == thr
            run = run_ref[...]
            sel = ((keys > thr) | (eq & (p_cur[...] + run < need))) & (keys != NEG_INF_KEY)
            run_ref[...] = run + jnp.sum(jnp.where(eq, 1.0, 0.0), axis=0, keepdims=True)
            v_t = vt_ref[0, k_cur, HEAD_DIM:2 * HEAD_DIM, :]
        if k_next is not None:
            s_next, p_next = bufs_next
            kch = akv_ref[0, chunk_rows(k_next), :]
        if k_next is not None:
            s_next[...] = _dot_nt(kch, q_z)
            eqf = jnp.where(keys_ref[k_next] == thr, 1.0, 0.0).astype(_MXU_DTYPE)
            p_next[...] = _dot(lower, eqf)
        if k_cur is not None:
            s = s_cur[...] if delta is None else s_cur[...] + bias_ref[delta]
            s = jnp.where(jnp.concatenate([sel] * H, axis=1), s, NEG)
            _online_t(s, m_ref, l_ref, acc_ref, v_t)

    _pair_loop(jnp.maximum(qi - 1, 0), step, (sa_ref, pa_ref), (sb_ref, pb_ref))
    _near_chunks(qi, step, (sa_ref, pa_ref), (sb_ref, pb_ref))

    o_ref[0] = _heads_to_rows_t(_finish_t(m_ref, l_ref, acc_ref), H).astype(o_ref.dtype)


def _dsa(apack, av_t, side, bias, cmask, T):
    Bn, S, _ = apack.shape
    topk = min(DSA_TOPK, S // 4)
    N = A_HEADS * T
    kern = functools.partial(_dsa_kernel, T=T, topk=topk)
    return pl.pallas_call(
        kern,
        grid=(Bn, S // T),
        in_specs=[pl.BlockSpec((1, T, 256), lambda b, i: (b, i, 0)),
                  pl.BlockSpec((1, S, 128), lambda b, i: (b, 0, 2)),
                  pl.BlockSpec((1,) + av_t.shape[1:], lambda b, i: (b, 0, 0, 0)),
                  pl.BlockSpec((1, T, 128), lambda b, i: (b, i, 3)),
                  pl.BlockSpec((1, S, 128), lambda b, i: (b, 0, 4)),
                  pl.BlockSpec((1, T, 128), lambda b, i: (b, i, 0)),
                  pl.BlockSpec(bias.shape, lambda b, i: (0, 0, 0)),
                  pl.BlockSpec(cmask.shape, lambda b, i: (0, 0))],
        out_specs=pl.BlockSpec((1, T, 256), lambda b, i: (b, i, 0)),
        out_shape=jax.ShapeDtypeStruct((Bn, S, 256), _MXU_DTYPE),
        scratch_shapes=[pltpu.VMEM((S // T, T, T), jnp.int32),
                        pltpu.VMEM((S // T, T, T), jnp.int16),
                        pltpu.VMEM((S // T, T, T), jnp.int16),
                        pltpu.VMEM((HEAD_DIM, N), F32),
                        pltpu.VMEM((1, N), F32), pltpu.VMEM((1, N), F32), pltpu.VMEM((1, T), F32),
                        pltpu.VMEM((T, N), F32), pltpu.VMEM((T, N), F32),
                        pltpu.VMEM((T, T), F32), pltpu.VMEM((T, T), F32)],
        compiler_params=_cparams(("parallel", "arbitrary")),
        name="dsa_attention",
    )(apack, apack, av_t, apack, apack, side, bias, cmask)


def _nsa_kernel(bq_ref, kvc_ref, vct_ref, kvs_ref, vst_ref, kvw_ref, vwt_ref, side_ref,
                bias_ref, wbias_ref, ovt_ref, o_ref, acc_ref, m_ref, l_ref, sa_ref, sb_ref,
                pa_ref, pb_ref, *, T, n_cmp, n_sel, n_win):
    qi = pl.program_id(1)
    nk = qi + 1
    H = B_HEADS
    NC = kvc_ref.shape[1]
    NS = ovt_ref.shape[0]
    log2_blk = int(math.log2(SLC_BLOCK))

    q_z = _zero_extended_queries(bq_ref[0], H, HEAD_DIM)
    t_q = qi * T + lax.broadcasted_iota(jnp.int32, (1, T), 1)

    def chunk_rows(kc):
        return pl.ds(pl.multiple_of(kc * T, T), T)

    n_idx = lax.broadcasted_iota(jnp.int32, (NC, T), 0)
    i_idx = lax.broadcasted_iota(jnp.int32, (NC, T), 1)
    cvalid = ((n_idx * CMP_STRIDE + (CMP_BLOCK - 1) - i_idx <= qi * T) & (n_idx < n_cmp))
    lc = jnp.where(jnp.concatenate([cvalid] * H, axis=1), _dot_nt(kvc_ref[0], q_z), NEG)
    mc = jnp.max(lc, axis=0, keepdims=True)
    ec = jnp.exp2(lc - mc)
    row_scale = jnp.where(mc > 0.5 * NEG,
                          1.0 / jnp.maximum(jnp.sum(ec, axis=0, keepdims=True), 1e-30), 0.0)
    pc = ec * row_scale
    o_cmp = _dot(vct_ref[0], pc.astype(_MXU_DTYPE))

    psum = pc[:, 0:T]
    for h in range(1, H):
        psum = psum + pc[:, h * T:(h + 1) * T]
    p_hi = psum.astype(_MXU_DTYPE)
    p_lo = (psum - p_hi.astype(F32)).astype(_MXU_DTYPE)
    imp = _dot(ovt_ref[...], p_hi) + _dot(ovt_ref[...], p_lo)

    blk = lax.broadcasted_iota(jnp.int32, (NS, T), 0)
    blkf = blk.astype(F32)
    cur = lax.shift_right_logical(t_q, log2_blk)
    forced = (blk == 0) | (blk == cur) | (blk == cur - 1)
    val = jnp.where(blk <= cur, jnp.where(forced, FORCE_SCORE, imp), -jnp.inf)

    def pick_one(_, carry):
        val, sel = carry
        mx = jnp.max(val, axis=0, keepdims=True)
        first = jnp.min(jnp.where(val == mx, blkf, float(NS)), axis=0, keepdims=True)
        pick = blkf == first
        sel = jnp.where(pick & (mx > -jnp.inf), 1.0, sel)
        return jnp.where(pick, -jnp.inf, val), sel

    _, sel = lax.fori_loop(0, n_sel, pick_one, (val, jnp.zeros((NS, T), F32)))
    sel = sel.astype(_MXU_DTYPE)

    e_row = lax.shift_right_logical(lax.broadcasted_iota(jnp.int32, (T, NS), 0), log2_blk)
    e_col = lax.broadcasted_iota(jnp.int32, (T, NS), 1)
    bpc = T // SLC_BLOCK

    _reset(m_ref, l_ref, acc_ref)

    def step(k_next, bufs_next, k_cur, bufs_cur, delta=None):
        if k_next is not None:
            s_next, p_next = bufs_next
            s_next[...] = _dot_nt(kvs_ref[0, chunk_rows(k_next), :], q_z)
            expand = jnp.where(e_col == k_next * bpc + e_row, 1.0, 0.0).astype(_MXU_DTYPE)
            p_next[...] = _dot(expand, sel)
        if k_cur is not None:
            s_cur, p_cur = bufs_cur
            addm = jnp.where(p_cur[...] > 0.5, 0.0, NEG)
            s = s_cur[...] + jnp.concatenate([addm] * H, axis=1)
            if delta is not None:
                s = s + bias_ref[delta]
            _online_t(s, m_ref, l_ref, acc_ref, vst_ref[0, k_cur, HEAD_DIM:2 * HEAD_DIM, :])

    _pair_loop(jnp.maximum(qi - 1, 0), step, (sa_ref, pa_ref), (sb_ref, pb_ref))
    _near_chunks(qi, step, (sa_ref, pa_ref), (sb_ref, pb_ref))
    o_slc = _finish_t(m_ref, l_ref, acc_ref)

    _reset(m_ref, l_ref, acc_ref)

    def produce_win(delta, buf):
        buf[...] = _dot_nt(kvw_ref[0, chunk_rows(jnp.maximum(qi - delta, 0)), :], q_z)

    def attend_win(delta, buf):
        kc = qi - delta
        s = buf[...] + wbias_ref[delta] + jnp.where(kc >= 0, 0.0, NEG)
        _online_t(s, m_ref, l_ref, acc_ref,
                  vwt_ref[0, jnp.maximum(kc, 0), HEAD_DIM:2 * HEAD_DIM, :])

    bufs = (sa_ref, sb_ref)
    produce_win(0, bufs[0])
    for delta in range(n_win):
        if delta + 1 < n_win:
            produce_win(delta + 1, bufs[(delta + 1) % 2])
        attend_win(delta, bufs[delta % 2])
    o_win = _finish_t(m_ref, l_ref, acc_ref)

    g = jax.nn.sigmoid(side_ref[0].T[IDX_HEADS:IDX_HEADS + 3 * H])
    outs = []
    for h in range(H):
        sl = slice(h * T, (h + 1) * T)
        outs.append(g[3 * h:3 * h + 1] * o_cmp[:, sl] + g[3 * h + 1:3 * h + 2] * o_slc[:, sl]
                    + g[3 * h + 2:3 * h + 3] * o_win[:, sl])
    o_ref[0] = jnp.concatenate(outs, axis=0).T.astype(o_ref.dtype)


def _nsa(bpack, kvc, vc_t, vs_t, vw_t, side, bias, wbias, overlap_t, T, n_cmp, n_sel):
    Bn, S, _ = bpack.shape
    NC = kvc.shape[1]
    N = B_HEADS * T
    kern = functools.partial(_nsa_kernel, T=T, n_cmp=n_cmp, n_sel=n_sel, n_win=wbias.shape[0])
    vtspec = lambda a: pl.BlockSpec((1,) + a.shape[1:], lambda b, i: (b, 0, 0, 0))
    return pl.pallas_call(
        kern,
        grid=(Bn, S // T),
        in_specs=[pl.BlockSpec((1, T, 256), lambda b, i: (b, i, 0)),
                  pl.BlockSpec((1, NC, 128), lambda b, i: (b, 0, 0)),
                  pl.BlockSpec((1, HEAD_DIM, NC), lambda b, i: (b, 0, 0)),
                  pl.BlockSpec((1, S, 128), lambda b, i: (b, 0, 3)),
                  vtspec(vs_t),
                  pl.BlockSpec((1, S, 128), lambda b, i: (b, 0, 4)),
                  vtspec(vw_t),
                  pl.BlockSpec((1, T, 128), lambda b, i: (b, i, 0)),
                  pl.BlockSpec(bias.shape, lambda b, i: (0, 0, 0)),
                  pl.BlockSpec(wbias.shape, lambda b, i: (0, 0, 0)),
                  pl.BlockSpec(overlap_t.shape, lambda b, i: (0, 0))],
        out_specs=pl.BlockSpec((1, T, 256), lambda b, i: (b, i, 0)),
        out_shape=jax.ShapeDtypeStruct((Bn, S, 256), _MXU_DTYPE),
        scratch_shapes=[pltpu.VMEM((HEAD_DIM, N), F32),
                        pltpu.VMEM((1, N), F32), pltpu.VMEM((1, N), F32),
                        pltpu.VMEM((T, N), F32), pltpu.VMEM((T, N), F32),
                        pltpu.VMEM((T, T), F32), pltpu.VMEM((T, T), F32)],
        compiler_params=_cparams(("parallel", "arbitrary")),
        name="nsa_attention",
    )(bpack, kvc, vc_t, bpack, vs_t, bpack, vw_t, side, bias, wbias, overlap_t)


def _diff_kernel(cq_ref, ck_ref, vt_ref, bias_ref, dl_ref, g_ref, o_ref, acc_ref, m_ref, l_ref,
                 sa_ref, sb_ref, *, T, lam_init):
    qi = pl.program_id(1)
    nk = qi + 1
    H = C_HEADS
    dl = dl_ref[...]
    lam = (jnp.exp(jnp.sum(dl[0:1] * dl[1:2], axis=1, keepdims=True))
           - jnp.exp(jnp.sum(dl[2:3] * dl[3:4], axis=1, keepdims=True)) + lam_init)

    cq = cq_ref[0]
    lane = lax.broadcasted_iota(jnp.int32, (T, 2 * HEAD_DIM), 1)
    q_z = []
    for h in range(H):
        qh = cq[:, h * 2 * HEAD_DIM:(h + 1) * 2 * HEAD_DIM]
        zero = jnp.zeros_like(qh)
        q_z.append(jnp.concatenate([jnp.where(lane < HEAD_DIM, qh, zero),
                                    jnp.where(lane >= HEAD_DIM, qh, zero)], axis=0))

    for h in range(H):
        _reset(m_ref.at[h], l_ref.at[h], acc_ref.at[h])

    def step(k_next, dst, k_cur, src, delta=None):
        if k_next is not None:
            kch = ck_ref[0, pl.ds(pl.multiple_of(k_next * T, T), T), :]
        if k_cur is not None:
            v_t = vt_ref[0, k_cur]
        for h in range(H):
            if k_next is not None:
                dst[h] = _dot_nt(kch[:, h * 2 * HEAD_DIM:(h + 1) * 2 * HEAD_DIM], q_z[h])
            if k_cur is not None:
                s = src[h]
                if delta is not None:
                    b = bias_ref[delta, h]
                    s = jnp.concatenate([s[:, 0:T] + b, s[:, T:2 * T] + b], axis=1)
                _online_t(s, m_ref.at[h], l_ref.at[h], acc_ref.at[h],
                          v_t[h * C_VDIM:(h + 1) * C_VDIM])

    _pair_loop(jnp.maximum(qi - 1, 0), step, sa_ref, sb_ref)
    _near_chunks(qi, step, sa_ref, sb_ref)

    outs = []
    for h in range(H):
        o = _finish_t(m_ref.at[h], l_ref.at[h], acc_ref.at[h])
        oc = o[:, 0:T] - lam * o[:, T:2 * T]
        rms = lax.rsqrt(jnp.mean(jnp.square(oc), axis=0, keepdims=True) + LN_EPS)
        outs.append(oc * rms * g_ref[...] * (1.0 - lam_init))
    o_ref[0] = jnp.concatenate(outs, axis=0).T.astype(o_ref.dtype)


def _diff(cpack, cv_t, bias, dl, g, T, lam_init):
    Bn, S, _ = cpack.shape
    W = C_HEADS * C_VDIM
    kern = functools.partial(_diff_kernel, T=T, lam_init=lam_init)
    return pl.pallas_call(
        kern,
        grid=(Bn, S // T),
        in_specs=[pl.BlockSpec((1, T, W), lambda b, i: (b, i, 0)),
                  pl.BlockSpec((1, S, W), lambda b, i: (b, 0, 1)),
                  pl.BlockSpec((1,) + cv_t.shape[1:], lambda b, i: (b, 0, 0, 0)),
                  pl.BlockSpec(bias.shape, lambda b, i: (0, 0, 0, 0)),
                  pl.BlockSpec(dl.shape, lambda b, i: (0, 0)),
                  pl.BlockSpec(g.shape, lambda b, i: (0, 0))],
        out_specs=pl.BlockSpec((1, T, W), lambda b, i: (b, i, 0)),
        out_shape=jax.ShapeDtypeStruct((Bn, S, W), _MXU_DTYPE),
        scratch_shapes=[pltpu.VMEM((C_HEADS, C_VDIM, 2 * T), F32),
                        pltpu.VMEM((C_HEADS, 1, 2 * T), F32),
                        pltpu.VMEM((C_HEADS, 1, 2 * T), F32),
                        pltpu.VMEM((C_HEADS, T, 2 * T), F32),
                        pltpu.VMEM((C_HEADS, T, 2 * T), F32)],
        compiler_params=_cparams(("parallel", "arbitrary")),
        name="diff_attention",
    )(cpack, cpack, cv_t, bias, dl, g)


def _layer_norm(v, g, b):
    mu = jnp.mean(v, axis=1, keepdims=True)
    d = v - mu
    var = jnp.mean(jnp.square(d), axis=1, keepdims=True)
    return d * lax.rsqrt(var + LN_EPS) * g + b


def _split_hi_lo(v):
    hi = v.astype(_MXU_DTYPE)
    lo = (v - hi.astype(F32)).astype(_MXU_DTYPE)
    return hi, lo


def _outproj_kernel(oa_ref, ob_ref, oc_ref, mg_ref, x_ref, mod1_ref, mod2_ref,
                    wa_ref, wb_ref, wc_ref, wo_ref, ln_ref, rw_ref, rb_ref,
                    x1_ref, h2_ref, tr_ref, tw_ref, cnt_ref, run_ref, *, dn_alpha, n_experts):
    D = x_ref.shape[2]
    tm = x_ref.shape[1]
    first_step = (pl.program_id(0) == 0) & (pl.program_id(1) == 0)

    @pl.when(first_step)
    def _():
        run_ref[...] = jnp.zeros(run_ref.shape, F32)

    ya = _dot(oa_ref[0], wa_ref[...])
    yb = _dot(ob_ref[0], wb_ref[...])
    yc = _dot(oc_ref[0], wc_ref[...])
    mg = mg_ref[0].astype(F32)
    merged = (jax.nn.sigmoid(mg[:, 0:D]) * ya + jax.nn.sigmoid(mg[:, D:2 * D]) * yb
              + jax.nn.sigmoid(mg[:, 2 * D:3 * D]) * yc)
    y = _dot(merged.astype(_MXU_DTYPE), wo_ref[...])
    gate1 = mod1_ref[0, 2:3, :]
    x1 = _layer_norm(dn_alpha * x_ref[0] + (1.0 + gate1) * y, ln_ref[0:1, :], ln_ref[1:2, :])
    x1_ref[0] = x1
    h2 = x1 * (1.0 + mod2_ref[0, 1:2, :]) + mod2_ref[0, 0:1, :]
    h2_ref[0] = h2.astype(h2_ref.dtype)

    h_hi, h_lo = _split_hi_lo(h2)
    w_hi, w_lo = _split_hi_lo(rw_ref[...])
    hw = _dot(h_hi, jnp.concatenate([w_hi, w_lo], axis=1))
    logits = hw[:, 0:LANES] + hw[:, LANES:2 * LANES] + _dot(h_lo, w_hi) + rb_ref[...]
    lane = lax.broadcasted_iota(jnp.int32, logits.shape, 1)
    lanef = lane.astype(F32)
    work = jnp.where(lane < n_experts, logits, -jnp.inf)
    tv = jnp.zeros(logits.shape, F32)
    firsts, onehots = [], []
    v0 = None
    for k in range(TOP_K):
        mx = jnp.max(work, axis=1, keepdims=True)
        first = jnp.min(jnp.where(work == mx, lanef, float(LANES)), axis=1, keepdims=True)
        if k == 0:
            v0 = mx
        tv = jnp.where(lane == k, jnp.exp(mx - v0), tv)
        hit = lanef == first
        work = jnp.where(hit, -jnp.inf, work)
        firsts.append(first)
        onehots.append(jnp.where(hit, 1.0, 0.0))
    tw_ref[0] = tv / jnp.sum(tv, axis=1, keepdims=True)

    chosen = onehots[0] + onehots[1] + onehots[2] + onehots[3]
    r_i = lax.broadcasted_iota(jnp.int32, (tm, tm), 0)
    c_i = lax.broadcasted_iota(jnp.int32, (tm, tm), 1)
    earlier = jnp.where(c_i < r_i, 1.0, 0.0).astype(_MXU_DTYPE)
    base = _dot(earlier, chosen.astype(_MXU_DTYPE)) + run_ref[0:1, :]
    tr = jnp.zeros(logits.shape, F32)
    for k in range(TOP_K):
        rank = jnp.sum(onehots[k] * base, axis=1, keepdims=True)
        tr = jnp.where(lane == k, firsts[k], tr)
        tr = jnp.where(lane == TOP_K + k, rank, tr)
    tr_ref[0] = tr.astype(jnp.int32)
    run_ref[...] = run_ref[...] + jnp.sum(chosen, axis=0, keepdims=True)
    cnt_ref[...] = run_ref[...]


def _outproj(oa, ob, oc, mg, x, mod1, mod2, wa, wb, wc, wo, ln, rw, rb, dn_alpha, n_experts):
    Bn, S, D = x.shape
    tm = TM_PROJ
    row = lambda w: pl.BlockSpec((1, tm, w), lambda b, i: (b, i, 0))
    full = lambda a: pl.BlockSpec(a.shape, lambda b, i: (0,) * a.ndim)
    modspec = pl.BlockSpec((1, 3, D), lambda b, i: (b, 0, 0))
    kern = functools.partial(_outproj_kernel, dn_alpha=dn_alpha, n_experts=n_experts)
    return pl.pallas_call(
        kern,
        grid=(Bn, S // tm),
        in_specs=[row(oa.shape[2]), row(ob.shape[2]), row(oc.shape[2]), row(mg.shape[2]), row(D),
                  modspec, modspec, full(wa), full(wb), full(wc), full(wo), full(ln), full(rw),
                  full(rb)],
        out_specs=[row(D), row(D), row(LANES), row(LANES),
                   pl.BlockSpec((SUBLANES, LANES), lambda b, i: (0, 0))],
        out_shape=[jax.ShapeDtypeStruct((Bn, S, D), F32),
                   jax.ShapeDtypeStruct((Bn, S, D), _MXU_DTYPE),
                   jax.ShapeDtypeStruct((Bn, S, LANES), jnp.int32),
                   jax.ShapeDtypeStruct((Bn, S, LANES), F32),
                   jax.ShapeDtypeStruct((SUBLANES, LANES), F32)],
        scratch_shapes=[pltpu.VMEM((SUBLANES, LANES), F32)],
        compiler_params=_cparams(("arbitrary", "arbitrary")),
        name="merge_outproj_ln_router",
    )(oa, ob, oc, mg, x, mod1, mod2, wa, wb, wc, wo, ln, rw, rb)


def _deinterleave_kernel(w_ref, p_ref, og_ref, ou_ref):
    n = w_ref.shape[1]
    for j in range(n // (2 * LANES)):
        blk = w_ref[:, j * 2 * LANES:(j + 1) * 2 * LANES].astype(_MXU_DTYPE)
        r = _dot(blk, p_ref[...])
        og_ref[:, j * LANES:(j + 1) * LANES] = r[:, 0:LANES].astype(og_ref.dtype)
        ou_ref[:, j * LANES:(j + 1) * LANES] = r[:, LANES:2 * LANES].astype(ou_ref.dtype)


def _deinterleave(w_all, part, n_parts):
    N2 = w_all.shape[1]
    R = w_all.shape[0] // n_parts
    tr = 1024
    first = part * (R // tr)
    perm = np.zeros((2 * LANES, 2 * LANES), np.float32)
    perm[2 * np.arange(LANES), np.arange(LANES)] = 1.0
    perm[2 * np.arange(LANES) + 1, LANES + np.arange(LANES)] = 1.0
    return pl.pallas_call(
        _deinterleave_kernel,
        grid=(R // tr,),
        in_specs=[pl.BlockSpec((tr, N2), lambda i: (first + i, 0)),
                  pl.BlockSpec(perm.shape, lambda i: (0, 0))],
        out_specs=[pl.BlockSpec((tr, N2 // 2), lambda i: (i, 0))] * 2,
        out_shape=[jax.ShapeDtypeStruct((R, N2 // 2), _MXU_DTYPE)] * 2,
        compiler_params=_cparams(("parallel",)),
        name="expert_weight_deinterleave",
    )(w_all, jnp.asarray(perm).astype(_MXU_DTYPE))


def _moe_kernel(te_ref, tv_ref, x_ref, wg_ref, wu_ref, wd_ref, bg_ref, bu_ref, bd_ref, o_ref):
    i = pl.program_id(0)

    @pl.when(tv_ref[i] > 0)
    def _():
        x = x_ref[...]
        dff = wg_ref.shape[2]
        ck = 512
        y = jnp.zeros(o_ref.shape, F32)
        for j in range(0, dff, ck):
            gate = jnp.minimum(_dot(x, wg_ref[0, :, j:j + ck]) + bg_ref[0, :, j:j + ck], SWIGLU_LIMIT)
            up = jnp.clip(_dot(x, wu_ref[0, :, j:j + ck]) + bu_ref[0, :, j:j + ck],
                          -SWIGLU_LIMIT, SWIGLU_LIMIT)
            act = (up + 1.0) * (gate * jax.nn.sigmoid(SWIGLU_ALPHA * gate))
            y = y + _dot(act.astype(_MXU_DTYPE), wd_ref[0, j:j + ck, :])
        o_ref[...] = (y + bd_ref[0]).astype(o_ref.dtype)


def _moe(xs, tile_expert, tile_valid, wg, wu, wd_all, wd_first, bg, bu, bd):
    R, D = xs.shape
    tm = TM_MOE
    nt = R // tm
    wspec = lambda a: pl.BlockSpec((1,) + a.shape[1:], lambda i, te, tv: (te[i], 0, 0))
    wdspec = pl.BlockSpec((1,) + wd_all.shape[1:], lambda i, te, tv: (wd_first + te[i], 0, 0))
    grid_spec = pltpu.PrefetchScalarGridSpec(
        num_scalar_prefetch=2,
        grid=(nt,),
        in_specs=[pl.BlockSpec((tm, D), lambda i, te, tv: (i, 0)),
                  wspec(wg), wspec(wu), wdspec, wspec(bg), wspec(bu), wspec(bd)],
        out_specs=pl.BlockSpec((tm, D), lambda i, te, tv: (i, 0)),
    )
    return pl.pallas_call(
        _moe_kernel,
        grid_spec=grid_spec,
        out_shape=jax.ShapeDtypeStruct((R, D), _MXU_DTYPE),
        compiler_params=_cparams(("arbitrary",)),
        name="moe_grouped_mlp",
    )(tile_expert, tile_valid, xs, wg, wu, wd_all, bg, bu, bd)


def _combine_kernel(yg_ref, tw_ref, x_ref, mod_ref, ln_ref, o_ref, *, dn_alpha):
    tw = tw_ref[0]
    y = tw[:, 0:1] * yg_ref[0, 0].astype(F32)
    for k in range(1, TOP_K):
        y = y + tw[:, k:k + 1] * yg_ref[k, 0].astype(F32)
    gate = mod_ref[0, 2:3, :]
    o_ref[0] = _layer_norm(dn_alpha * x_ref[0] + (1.0 + gate) * y, ln_ref[0:1, :], ln_ref[1:2, :])


def _combine(yg, tw, x, mod, ln, dn_alpha):
    Bn, S, D = x.shape
    tm = TM_PROJ
    row = lambda w: pl.BlockSpec((1, tm, w), lambda b, i: (b, i, 0))
    return pl.pallas_call(
        functools.partial(_combine_kernel, dn_alpha=dn_alpha),
        grid=(Bn, S // tm),
        in_specs=[pl.BlockSpec((TOP_K, 1, tm, D), lambda b, i: (0, b, i, 0)), row(LANES), row(D),
                  pl.BlockSpec((1, 3, D), lambda b, i: (b, 0, 0)),
                  pl.BlockSpec(ln.shape, lambda b, i: (0, 0))],
        out_specs=row(D),
        out_shape=jax.ShapeDtypeStruct((Bn, S, D), F32),
        compiler_params=_cparams(("parallel", "parallel")),
        name="moe_combine_ln",
    )(yg, tw, x, mod, ln)


def _bucket_np(dist):
    exact = N_BUCKETS // 2
    n = np.maximum(dist, 0)
    nf = np.maximum(n, 1).astype(np.float32)
    large = exact + (np.log(nf / exact) / math.log(MAX_DISTANCE / exact)
                     * (N_BUCKETS - exact)).astype(np.int32)
    large = np.minimum(large, N_BUCKETS - 1)
    return np.where(n < exact, n, large)


def _bias_tiles_t(tab, T, n_delta, window=None, shift_far=True):
    H = tab.shape[1]
    L = 2 * T
    k = np.arange(L)
    diff = np.where(k < T, k, k - L)
    d = diff[None, :] + (np.arange(n_delta) * T)[:, None]
    ok = d >= 0 if window is None else (d >= 0) & (d < window)
    onehot = np.eye(N_BUCKETS, dtype=np.float32)[_bucket_np(d)]
    prof = jnp.einsum("dlb,bh->dhl", jnp.asarray(onehot), tab, precision=lax.Precision.HIGHEST)
    if shift_far:
        prof = prof - tab[N_BUCKETS - 1][None, :, None]
    prof = jnp.where(jnp.asarray(ok)[:, None, :], prof * LOG2E, NEG)
    rolled = jnp.tile(prof, (1, 1, T))[:, :, :T * (L - 1)].reshape(n_delta, H, T, L - 1)
    return rolled[:, :, :, :T]


def _stack_tiles(tiles):
    n, H, T, _ = tiles.shape
    return jnp.transpose(tiles, (0, 2, 1, 3)).reshape(n, T, H * T)


def _slc_overlap_np(n_cmp, n_slc, rows):
    start = np.arange(n_cmp) * CMP_STRIDE
    end = start + CMP_BLOCK
    bs = np.arange(n_slc) * SLC_BLOCK
    ov = (start[:, None] < bs[None, :] + SLC_BLOCK) & (end[:, None] > bs[None, :])
    out = np.zeros((rows, n_slc), np.float32)
    out[:n_cmp] = ov
    return out


def _routing(top_i, rank, counts, tm):
    N = top_i.shape[0]
    E = counts.shape[0]
    A = N * TOP_K
    nt = A // tm + E
    padded = ((counts + tm - 1) // tm) * tm
    starts = jnp.cumsum(counts) - counts
    pstarts = jnp.cumsum(padded) - padded
    pend = pstarts + padded
    tile_start = jnp.arange(nt, dtype=jnp.int32) * tm
    tile_valid = (tile_start < pend[-1]).astype(jnp.int32)
    te = jnp.minimum(jnp.sum((tile_start[:, None] >= pend[None, :]).astype(jnp.int32), axis=1), E - 1)
    last_valid = jnp.max(jnp.where(tile_valid > 0, te, 0))
    tile_expert = jnp.where(tile_valid > 0, te, last_valid).astype(jnp.int32)
    onehot_t = tile_expert[:, None] == jnp.arange(E)[None, :]
    pick = lambda tab: jnp.sum(jnp.where(onehot_t, tab[None, :], 0), axis=1)
    order = jnp.argsort(top_i.reshape(A), stable=True).astype(jnp.int32)
    in_group = (tile_start - pick(pstarts))[:, None] + jnp.arange(tm, dtype=jnp.int32)[None, :]
    ok = (in_group < pick(counts)[:, None]) & (tile_valid[:, None] > 0)
    slot = jnp.clip(pick(starts)[:, None] + in_group, 0, A - 1)
    filler = (jnp.arange(nt * tm, dtype=jnp.int32) % N).reshape(nt, tm)
    row_tok = jnp.where(ok, jnp.take(order, slot.reshape(-1), mode="clip").reshape(nt, tm) // TOP_K,
                        filler)
    onehot_a = top_i[:, :, None] == jnp.arange(E)[None, None, :]
    pos = rank + jnp.sum(jnp.where(onehot_a, pstarts[None, None, :], 0), axis=2)
    return tile_expert, tile_valid, row_tok.reshape(nt * tm), pos.astype(jnp.int32)


def kernel(x, c, rel_bias, mod_attn_w, mod_attn_b, w_in, cmp_pos, cmp_w1, cmp_w2, diff_lambda,
           diff_norm_g, w_branch_a, w_branch_b, w_branch_c, w_out, ln1_g, ln1_b, mod_ffn_w,
           mod_ffn_b, router_w, router_b, exp_w_gu, exp_b_gu, exp_w_down, exp_b_down, ln2_g, ln2_b):
    Bn, S, D = x.shape
    L = w_in.shape[0]
    E = exp_w_gu.shape[1]
    T = T_ATT
    cdt = _MXU_DTYPE
    dn_alpha = (2 * L) ** 0.25
    assert S % T == 0 and S % TM_PROJ == 0 and (Bn * S * TOP_K) % TM_MOE == 0
    assert (S // SLC_BLOCK) % SUBLANES == 0 and E <= LANES and T >= MAX_DISTANCE

    mod_w = jnp.stack([mod_attn_w, mod_ffn_w], axis=1).reshape(2 * L, D, 3 * D)
    mod_b = jnp.stack([mod_attn_b, mod_ffn_b], axis=1).reshape(2 * L, 3 * D)
    mods = _adaln(c, mod_w, mod_b).reshape(2 * L, Bn, 3, D)

    n_win = WINDOW // T + 1
    tab_b = rel_bias[:, A_HEADS:A_HEADS + B_HEADS]
    bias_a = _stack_tiles(_bias_tiles_t(rel_bias[:, :A_HEADS], T, 2))
    bias_b = _stack_tiles(_bias_tiles_t(tab_b, T, 2))
    bias_w = _stack_tiles(_bias_tiles_t(tab_b, T, n_win, WINDOW, shift_far=False))
    bias_c = _bias_tiles_t(rel_bias[:, A_HEADS + B_HEADS:], T, 2)
    jj = np.arange(T)
    cmask = jnp.asarray(np.where(jj[None, :] >= jj[:, None], 0.0, -np.inf).astype(np.float32))

    n_cmp = (S - CMP_BLOCK) // CMP_STRIDE + 1
    n_slc = S // SLC_BLOCK
    n_sel = min(SLC_TOPN, n_slc)
    NC = S // CMP_STRIDE
    overlap_t = jnp.asarray(_slc_overlap_np(n_cmp, n_slc, NC).T).astype(cdt)

    rows16 = CMP_STRIDE * 2 * HEAD_DIM
    w_gu_all = exp_w_gu.reshape(L * E * D, -1)
    w_down_all = exp_w_down.astype(cdt).reshape(L * E, exp_w_down.shape[2], D)

    for l in range(L):
        lam_init = 0.8 - 0.6 * math.exp(-0.3 * l)
        wl = w_in[l]
        wa = jnp.pad(jnp.concatenate([wl[:, 0:256] * Q_FOLD, wl[:, 256:544]], axis=1),
                     ((0, 0), (0, 96))).astype(cdt)
        wb = jnp.concatenate([wl[:, 548:804] * Q_FOLD, wl[:, 804:1188]], axis=1).astype(cdt)
        wc = jnp.concatenate([wl[:, 1200:1712] * Q_FOLD, wl[:, 1712:2736]], axis=1).astype(cdt)
        ws = jnp.pad(jnp.concatenate([wl[:, 544:548], wl[:, 1188:1200]], axis=1),
                     ((0, 0), (0, LANES - 16))).astype(cdt)
        wg = wl[:, 2736:].astype(cdt)

        apack, bpack, cpack, side, mg, akv_t, kvs_t, kvw_t, cv_t = _inproj(
            x, mods[2 * l], wa, wb, wc, ws, wg, T)

        r2 = bpack[:, :, 256:384].reshape(Bn, NC, rows16)
        r2s = jnp.concatenate([r2[:, 1:], jnp.zeros_like(r2[:, :1])], axis=1)
        half = CMP_BLOCK // 2
        zpad = jnp.zeros((half, HEAD_DIM), F32)
        pa, pb, wca, wcb = [], [], [], []
        for j in range(2):
            pos = cmp_pos[l, j]
            w1 = cmp_w1[l, j].reshape(CMP_BLOCK, HEAD_DIM, CMP_HIDDEN)
            zw = jnp.zeros((half, HEAD_DIM, CMP_HIDDEN), F32)
            kv = (lambda a, z: jnp.concatenate([a, z], axis=1)) if j == 0 else \
                 (lambda a, z: jnp.concatenate([z, a], axis=1))
            pa.append(kv(pos[:half], zpad).reshape(1, rows16))
            pb.append(kv(pos[half:], zpad).reshape(1, rows16))
            wca.append(kv(w1[:half], zw).reshape(rows16, CMP_HIDDEN))
            wcb.append(kv(w1[half:], zw).reshape(rows16, CMP_HIDDEN))
        kvc = _compress(r2, r2s, jnp.stack(pa), jnp.stack(pb), jnp.stack(wca).astype(cdt),
                        jnp.stack(wcb).astype(cdt), cmp_w2[l].astype(cdt))

        oa = _dsa(apack, akv_t, side, bias_a, cmask, T)
        ob = _nsa(bpack, kvc, jnp.transpose(kvc[:, :, HEAD_DIM:], (0, 2, 1)), kvs_t, kvw_t,
                  side, bias_b, bias_w, overlap_t, T, n_cmp, n_sel)
        oc = _diff(cpack, cv_t, bias_c, diff_lambda[l], diff_norm_g[l].reshape(C_VDIM, 1), T,
                   lam_init)

        rw = jnp.pad(router_w[l], ((0, 0), (0, LANES - E)))
        rb = jnp.pad(router_b[l], (0, LANES - E)).reshape(1, LANES)
        x1, h2, tr, tw, cnt = _outproj(
            oa, ob, oc, mg, x, mods[2 * l], mods[2 * l + 1],
            w_branch_a[l].astype(cdt), w_branch_b[l].astype(cdt), w_branch_c[l].astype(cdt),
            w_out[l].astype(cdt), jnp.stack([ln1_g[l], ln1_b[l]]), rw, rb, dn_alpha, E)

        N = Bn * S
        tr = tr.reshape(N, LANES)
        tile_expert, tile_valid, row_tok, pos = _routing(
            tr[:, 0:TOP_K], tr[:, TOP_K:2 * TOP_K], cnt[0, :E].astype(jnp.int32), TM_MOE)
        xs = jnp.take(h2.reshape(N, D), row_tok, axis=0, mode="clip")
        wgate, wup = _deinterleave(w_gu_all, l, L)
        bgu = exp_b_gu[l].reshape(E, 1, -1, 2)
        ys = _moe(xs, tile_expert, tile_valid,
                  wgate.reshape(E, D, -1), wup.reshape(E, D, -1), w_down_all, l * E,
                  bgu[..., 0], bgu[..., 1], exp_b_down[l].reshape(E, 1, D))
        yg = jnp.take(ys, pos.T.reshape(-1), axis=0, mode="clip").reshape(TOP_K, Bn, S, D)
        x = _combine(yg, tw, x1, mods[2 * l + 1], jnp.stack([ln2_g[l], ln2_b[l]]), dn_alpha)
    return x
```

```python
import functools
import math

import numpy as np
import jax
import jax.numpy as jnp
from jax import lax
from jax.experimental import pallas as pl
from jax.experimental.pallas import tpu as pltpu

F32 = jnp.float32
_MXU_DTYPE = jnp.bfloat16

HEAD_DIM = 64
A_HEADS = 4
IDX_HEADS = 4
IDX_DIM = 32
DSA_TOPK = 256
B_HEADS = 4
CMP_BLOCK = 32
CMP_STRIDE = 16
CMP_HIDDEN = 256
SLC_BLOCK = 64
SLC_TOPN = 16
WINDOW = 512
FORCE_SCORE = 1e9
C_HEADS = 4
C_VDIM = 2 * HEAD_DIM
N_BUCKETS = 32
MAX_DISTANCE = 128
TOP_K = 4
SWIGLU_LIMIT = 7.0
SWIGLU_ALPHA = 1.702
LN_EPS = 1e-5

LOG2E = math.log2(math.e)
Q_FOLD = HEAD_DIM ** -0.5 * LOG2E
NEG = -1e30
INT_MIN = -2 ** 31
NEG_INF_KEY = INT_MIN + 0x7FFFFF

LANES = 128
SUBLANES = 8
VMEM_LIMIT_BYTES = 56 * 1024 * 1024

T_ATT = 256
TM_PROJ = 512
TM_MOE = 512


def _cparams(sem):
    return pltpu.CompilerParams(dimension_semantics=sem, vmem_limit_bytes=VMEM_LIMIT_BYTES)


def _dot(a, b):
    return jnp.dot(a, b, preferred_element_type=F32)


def _dot_nt(a, b):
    return lax.dot_general(a, b, (((1,), (1,)), ((), ())), preferred_element_type=F32)


def _adaln_kernel(c_ref, w_ref, b_ref, o_ref):
    c = c_ref[...]
    a = (c * jax.nn.sigmoid(c)).astype(_MXU_DTYPE)
    o_ref[0] = _dot(a, w_ref[0].astype(_MXU_DTYPE)) + b_ref[0]


def _adaln(c, w, b):
    G, D, N = w.shape
    Bn = c.shape[0]
    tn = 768
    return pl.pallas_call(
        _adaln_kernel,
        grid=(G, N // tn),
        in_specs=[
            pl.BlockSpec((Bn, D), lambda g, j: (0, 0)),
            pl.BlockSpec((1, D, tn), lambda g, j: (g, 0, j)),
            pl.BlockSpec((1, 1, tn), lambda g, j: (g, 0, j)),
        ],
        out_specs=pl.BlockSpec((1, Bn, tn), lambda g, j: (g, 0, j)),
        out_shape=jax.ShapeDtypeStruct((G, Bn, N), F32),
        compiler_params=_cparams(("parallel", "parallel")),
        name="adaln",
    )(c, w, b.reshape(G, 1, N))


_T_COLS = ((0, 256, 128), (1, 384, 128), (1, 512, 128), (2, 1024, 512))


def _inproj_kernel(x_ref, mod_ref, wa_ref, wb_ref, wc_ref, ws_ref, wg_ref,
                   oa_ref, ob_ref, oc_ref, os_ref, og_ref, *t_refs):
    x = x_ref[0]
    tm = x.shape[0]
    shift = mod_ref[0, 0:1, :]
    scale = mod_ref[0, 1:2, :]
    h = (x * (1.0 + scale) + shift).astype(_MXU_DTYPE)
    for g, (w_ref, o_ref) in enumerate(((wa_ref, oa_ref), (wb_ref, ob_ref), (wc_ref, oc_ref),
                                        (ws_ref, os_ref), (wg_ref, og_ref))):
        n = w_ref.shape[1]
        for j in range(0, n, 512):
            w = min(512, n - j)
            r = _dot(h, w_ref[:, j:j + w])
            o_ref[0, :, j:j + w] = r.astype(o_ref.dtype)
            for (tg, c0, tw), t_ref in zip(_T_COLS, t_refs):
                if tg == g and j <= c0 and c0 + tw <= j + w:
                    T = t_ref.shape[3]
                    for c in range(tm // T):
                        blk = r[c * T:(c + 1) * T, c0 - j:c0 - j + tw]
                        t_ref[0, c] = blk.T.astype(t_ref.dtype)


def _inproj(x, mod, wa, wb, wc, ws, wg, T):
    Bn, S, D = x.shape
    tm = TM_PROJ
    ws_ = [wa, wb, wc, ws, wg]
    dts = [_MXU_DTYPE, _MXU_DTYPE, _MXU_DTYPE, F32, _MXU_DTYPE]
    return pl.pallas_call(
        _inproj_kernel,
        grid=(Bn, S // tm),
        in_specs=[pl.BlockSpec((1, tm, D), lambda b, i: (b, i, 0)),
                  pl.BlockSpec((1, 3, D), lambda b, i: (b, 0, 0))]
                 + [pl.BlockSpec(w.shape, lambda b, i: (0, 0)) for w in ws_],
        out_specs=[pl.BlockSpec((1, tm, w.shape[1]), lambda b, i: (b, i, 0)) for w in ws_]
                  + [pl.BlockSpec((1, tm // T, tw, T), lambda b, i: (b, i, 0, 0))
                     for _, _, tw in _T_COLS],
        out_shape=[jax.ShapeDtypeStruct((Bn, S, w.shape[1]), dt) for w, dt in zip(ws_, dts)]
                  + [jax.ShapeDtypeStruct((Bn, S // T, tw, T), _MXU_DTYPE) for _, _, tw in _T_COLS],
        compiler_params=_cparams(("parallel", "parallel")),
        name="inproj",
    )(x, mod, *ws_)


def _compress_kernel(r_ref, rs_ref, pa_ref, pb_ref, wa_ref, wb_ref, w2_ref, o_ref):
    r = r_ref[0].astype(F32)
    rs = rs_ref[0].astype(F32)
    outs = []
    for j in range(2):
        xa = (r + pa_ref[j]).astype(_MXU_DTYPE)
        xb = (rs + pb_ref[j]).astype(_MXU_DTYPE)
        hid = jax.nn.gelu(_dot(xa, wa_ref[j]) + _dot(xb, wb_ref[j]))
        outs.append(_dot(hid.astype(_MXU_DTYPE), w2_ref[j]))
    o_ref[0] = jnp.concatenate(outs, axis=1).astype(o_ref.dtype)


def _compress(r2, r2s, pa, pb, wa, wb, w2):
    Bn, NC, KW = r2.shape
    return pl.pallas_call(
        _compress_kernel,
        grid=(Bn,),
        in_specs=[pl.BlockSpec((1, NC, KW), lambda b: (b, 0, 0)),
                  pl.BlockSpec((1, NC, KW), lambda b: (b, 0, 0)),
                  pl.BlockSpec(pa.shape, lambda b: (0, 0, 0)),
                  pl.BlockSpec(pb.shape, lambda b: (0, 0, 0)),
                  pl.BlockSpec(wa.shape, lambda b: (0, 0, 0)),
                  pl.BlockSpec(wb.shape, lambda b: (0, 0, 0)),
                  pl.BlockSpec(w2.shape, lambda b: (0, 0, 0))],
        out_specs=pl.BlockSpec((1, NC, 2 * HEAD_DIM), lambda b: (b, 0, 0)),
        out_shape=jax.ShapeDtypeStruct((Bn, NC, 2 * HEAD_DIM), _MXU_DTYPE),
        compiler_params=_cparams(("parallel",)),
        name="nsa_compress",
    )(r2, r2s, pa, pb, wa, wb, w2)


def _online_t(s, m_ref, l_ref, acc_ref, v_t):
    m = m_ref[...]
    m_new = jnp.maximum(m, jnp.max(s, axis=0, keepdims=True))
    alpha = jnp.exp2(m - m_new)
    p = jnp.exp2(s - m_new)
    l_ref[...] = alpha * l_ref[...] + jnp.sum(p, axis=0, keepdims=True)
    acc_ref[...] = alpha * acc_ref[...] + _dot(v_t, p.astype(_MXU_DTYPE))
    m_ref[...] = m_new


def _reset(m_ref, l_ref, acc_ref):
    m_ref[...] = jnp.full(m_ref.shape, NEG, F32)
    l_ref[...] = jnp.zeros(l_ref.shape, F32)
    acc_ref[...] = jnp.zeros(acc_ref.shape, F32)


def _finish_t(m_ref, l_ref, acc_ref):
    return jnp.where(m_ref[...] > 0.5 * NEG, acc_ref[...] / jnp.maximum(l_ref[...], 1e-30), 0.0)


def _pair_loop(n, step, buf_a, buf_b):
    @pl.when(n > 0)
    def _():
        step(0, buf_a, None, None)

    def pair(i, carry):
        k0 = 2 * i
        step(jnp.minimum(k0 + 1, n - 1), buf_b, k0, buf_a)

        @pl.when(k0 + 1 < n)
        def _():
            step(jnp.minimum(k0 + 2, n - 1), buf_a, k0 + 1, buf_b)
        return carry

    lax.fori_loop(0, (n + 1) // 2, pair, 0)


def _near_chunks(qi, step, buf_a, buf_b):
    @pl.when(qi >= 1)
    def _():
        step(qi - 1, buf_a, None, None)
        step(qi, buf_b, qi - 1, buf_a, 1)

    @pl.when(qi == 0)
    def _():
        step(qi, buf_b, None, None)

    step(None, None, qi, buf_b, 0)


def _zero_extended_queries(q, n_heads, width):
    t = q.shape[0]
    pad = jnp.zeros((t, LANES - width), q.dtype)
    return jnp.concatenate(
        [jnp.concatenate([q[:, h * width:(h + 1) * width], pad], axis=1) for h in range(n_heads)],
        axis=0)


def _heads_to_rows_t(o_t, n_heads):
    t = o_t.shape[1] // n_heads
    stacked = jnp.concatenate([o_t[:, h * t:(h + 1) * t] for h in range(n_heads)], axis=0)
    return stacked.T


def _dsa_kernel(aq_ref, akv_ref, vt_ref, iq_ref, ik_ref, side_ref, bias_ref, cmask_ref, o_ref,
                keys_ref, hi_ref, lo_ref, acc_ref, m_ref, l_ref, run_ref, sa_ref, sb_ref, pa_ref,
                pb_ref, *, T, topk):
    qi = pl.program_id(1)
    nk = qi + 1
    H = A_HEADS

    q_z = _zero_extended_queries(aq_ref[0], H, HEAD_DIM)
    iq_z = _zero_extended_queries(iq_ref[0], IDX_HEADS, IDX_DIM)
    iw_t = side_ref[0].T[0:IDX_HEADS] * (IDX_HEADS ** -0.5 * IDX_DIM ** -0.5)

    def chunk_rows(kc):
        return pl.ds(pl.multiple_of(kc * T, T), T)

    def raw_scores(kc):
        return _dot_nt(ik_ref[0, chunk_rows(kc), :], iq_z)

    def to_keys(kc, raw, causal_mask):
        sc = jnp.maximum(raw, 0.0)
        isc = iw_t[0:1] * sc[:, 0:T]
        for h in range(1, IDX_HEADS):
            isc = isc + iw_t[h:h + 1] * sc[:, h * T:(h + 1) * T]
        isc = isc + 0.0
        if causal_mask is not None:
            isc = isc + causal_mask
        bits = pltpu.bitcast(isc, jnp.int32)
        keys = jnp.where(bits < 0, bits ^ 0x7FFFFFFF, bits)
        keys_ref[kc] = keys
        hi_ref[kc] = lax.shift_right_arithmetic(keys, 16).astype(jnp.int16)

    def far_keys(i, carry):
        k0 = 2 * i
        k1 = jnp.minimum(k0 + 1, qi - 1)
        raw0 = raw_scores(k0)
        raw1 = raw_scores(k1)
        to_keys(k0, raw0, None)
        to_keys(k1, raw1, None)
        return carry

    lax.fori_loop(0, (qi + 1) // 2, far_keys, 0)
    to_keys(qi, raw_scores(qi), cmask_ref[...])

    n_acc = 4
    half_rows = 2 * SUBLANES

    def count16(plane_ref, cand):
        cand16 = cand.astype(jnp.int16)
        one, zero = jnp.int16(1), jnp.int16(0)

        def body(kc, accs):
            k = plane_ref[kc]
            hit = jnp.where(k >= cand16, one, zero)
            parts = hit.reshape(T // half_rows, half_rows, T)
            accs = list(accs)
            for r in range(T // half_rows):
                accs[r % n_acc] = accs[r % n_acc] + parts[r]
            return tuple(accs)

        accs = lax.fori_loop(0, nk, body,
                             tuple(jnp.zeros((half_rows, T), jnp.int16) for _ in range(n_acc)))
        total = (accs[0] + accs[1]) + (accs[2] + accs[3])
        return jnp.sum(total.astype(F32), axis=0, keepdims=True)

    def search16(plane_ref, base_count):
        def bit_step(b, carry):
            thr, above = carry
            cand = thr + lax.shift_left(jnp.int32(1), 15 - b)
            cnt = base_count + count16(plane_ref, cand)
            take = cnt >= float(topk)
            return jnp.where(take, cand, thr), jnp.where(take, above, cnt)
        init = (jnp.full((1, T), -2 ** 15, jnp.int32), jnp.zeros((1, T), F32) + base_count)
        return lax.fori_loop(0, 16, bit_step, init)

    thr_hi, above_hi = search16(hi_ref, 0.0)

    def low_plane(kc, carry):
        keys = keys_ref[kc]
        lo = (keys & 0xFFFF) - 2 ** 15
        same = lax.shift_right_arithmetic(keys, 16) == thr_hi
        lo_ref[kc] = jnp.where(same, lo, -2 ** 15).astype(jnp.int16)
        return carry

    lax.fori_loop(0, nk, low_plane, 0)
    thr_lo, above = search16(lo_ref, above_hi)
    thr = thr_hi * 2 ** 16 + (thr_lo + 2 ** 15)
    need = float(topk) - above

    row = lax.broadcasted_iota(jnp.int32, (T, T), 0)
    col = lax.broadcasted_iota(jnp.int32, (T, T), 1)
    lower = jnp.where(col < row, 1.0, 0.0).astype(_MXU_DTYPE)

    _reset(m_ref, l_ref, acc_ref)
    run_ref[...] = jnp.zeros(run_ref.shape, F32)

    def step(k_next, bufs_next, k_cur, bufs_cur, delta=None):
        if k_cur is not None:
            s_cur, p_cur = bufs_cur
            keys = keys_ref[k_cur]
            eq = keys == thr
            run = run_ref[...]
            sel = ((keys > thr) | (eq & (p_cur[...] + run < need))) & (keys != NEG_INF_KEY)
            run_ref[...] = run + jnp.sum(jnp.where(eq, 1.0, 0.0), axis=0, keepdims=True)
            v_t = vt_ref[0, k_cur, HEAD_DIM:2 * HEAD_DIM, :]
        if k_next is not None:
            s_next, p_next = bufs_next
            kch = akv_ref[0, chunk_rows(k_next), :]
        if k_next is not None:
            s_next[...] = _dot_nt(kch, q_z)
            eqf = jnp.where(keys_ref[k_next] == thr, 1.0, 0.0).astype(_MXU_DTYPE)
            p_next[...] = _dot(lower, eqf)
        if k_cur is not None:
            s = s_cur[...] if delta is None else s_cur[...] + bias_ref[delta]
            s = jnp.where(jnp.concatenate([sel] * H, axis=1), s, NEG)
            _online_t(s, m_ref, l_ref, acc_ref, v_t)

    _pair_loop(jnp.maximum(qi - 1, 0), step, (sa_ref, pa_ref), (sb_ref, pb_ref))
    _near_chunks(qi, step, (sa_ref, pa_ref), (sb_ref, pb_ref))

    o_ref[0] = _heads_to_rows_t(_finish_t(m_ref, l_ref, acc_ref), H).astype(o_ref.dtype)


def _dsa(apack, av_t, side, bias, cmask, T):
    Bn, S, _ = apack.shape
    topk = min(DSA_TOPK, S // 4)
    N = A_HEADS * T
    kern = functools.partial(_dsa_kernel, T=T, topk=topk)
    return pl.pallas_call(
        kern,
        grid=(Bn, S // T),
        in_specs=[pl.BlockSpec((1, T, 256), lambda b, i: (b, i, 0)),
                  pl.BlockSpec((1, S, 128), lambda b, i: (b, 0, 2)),
                  pl.BlockSpec((1,) + av_t.shape[1:], lambda b, i: (b, 0, 0, 0)),
                  pl.BlockSpec((1, T, 128), lambda b, i: (b, i, 3)),
                  pl.BlockSpec((1, S, 128), lambda b, i: (b, 0, 4)),
                  pl.BlockSpec((1, T, 128), lambda b, i: (b, i, 0)),
                  pl.BlockSpec(bias.shape, lambda b, i: (0, 0, 0)),
                  pl.BlockSpec(cmask.shape, lambda b, i: (0, 0))],
        out_specs=pl.BlockSpec((1, T, 256), lambda b, i: (b, i, 0)),
        out_shape=jax.ShapeDtypeStruct((Bn, S, 256), _MXU_DTYPE),
        scratch_shapes=[pltpu.VMEM((S // T, T, T), jnp.int32),
                        pltpu.VMEM((S // T, T, T), jnp.int16),
                        pltpu.VMEM((S // T, T, T), jnp.int16),
                        pltpu.VMEM((HEAD_DIM, N), F32),
                        pltpu.VMEM((1, N), F32), pltpu.VMEM((1, N), F32), pltpu.VMEM((1, T), F32),
                        pltpu.VMEM((T, N), F32), pltpu.VMEM((T, N), F32),
                        pltpu.VMEM((T, T), F32), pltpu.VMEM((T, T), F32)],
        compiler_params=_cparams(("parallel", "arbitrary")),
        name="dsa_attention",
    )(apack, apack, av_t, apack, apack, side, bias, cmask)


def _nsa_kernel(bq_ref, kvc_ref, vct_ref, kvs_ref, vst_ref, kvw_ref, vwt_ref, side_ref,
                bias_ref, wbias_ref, ovt_ref, o_ref, acc_ref, m_ref, l_ref, sa_ref, sb_ref,
                pa_ref, pb_ref, *, T, n_cmp, n_sel, n_win):
    qi = pl.program_id(1)
    nk = qi + 1
    H = B_HEADS
    NC = kvc_ref.shape[1]
    NS = ovt_ref.shape[0]
    log2_blk = int(math.log2(SLC_BLOCK))

    q_z = _zero_extended_queries(bq_ref[0], H, HEAD_DIM)
    t_q = qi * T + lax.broadcasted_iota(jnp.int32, (1, T), 1)

    def chunk_rows(kc):
        return pl.ds(pl.multiple_of(kc * T, T), T)

    n_idx = lax.broadcasted_iota(jnp.int32, (NC, T), 0)
    i_idx = lax.broadcasted_iota(jnp.int32, (NC, T), 1)
    cvalid = ((n_idx * CMP_STRIDE + (CMP_BLOCK - 1) - i_idx <= qi * T) & (n_idx < n_cmp))
    lc = jnp.where(jnp.concatenate([cvalid] * H, axis=1), _dot_nt(kvc_ref[0], q_z), NEG)
    mc = jnp.max(lc, axis=0, keepdims=True)
    ec = jnp.exp2(lc - mc)
    row_scale = jnp.where(mc > 0.5 * NEG,
                          1.0 / jnp.maximum(jnp.sum(ec, axis=0, keepdims=True), 1e-30), 0.0)
    pc = ec * row_scale
    o_cmp = _dot(vct_ref[0], pc.astype(_MXU_DTYPE))

    psum = pc[:, 0:T]
    for h in range(1, H):
        psum = psum + pc[:, h * T:(h + 1) * T]
    p_hi = psum.astype(_MXU_DTYPE)
    p_lo = (psum - p_hi.astype(F32)).astype(_MXU_DTYPE)
    imp = _dot(ovt_ref[...], p_hi) + _dot(ovt_ref[...], p_lo)

    blk = lax.broadcasted_iota(jnp.int32, (NS, T), 0)
    blkf = blk.astype(F32)
    cur = lax.shift_right_logical(t_q, log2_blk)
    forced = (blk == 0) | (blk == cur) | (blk == cur - 1)
    val = jnp.where(blk <= cur, jnp.where(forced, FORCE_SCORE, imp), -jnp.inf)

    def pick_one(_, carry):
        val, sel = carry
        mx = jnp.max(val, axis=0, keepdims=True)
        first = jnp.min(jnp.where(val == mx, blkf, float(NS)), axis=0, keepdims=True)
        pick = blkf == first
        sel = jnp.where(pick & (mx > -jnp.inf), 1.0, sel)
        return jnp.where(pick, -jnp.inf, val), sel

    _, sel = lax.fori_loop(0, n_sel, pick_one, (val, jnp.zeros((NS, T), F32)))
    sel = sel.astype(_MXU_DTYPE)

    e_row = lax.shift_right_logical(lax.broadcasted_iota(jnp.int32, (T, NS), 0), log2_blk)
    e_col = lax.broadcasted_iota(jnp.int32, (T, NS), 1)
    bpc = T // SLC_BLOCK

    _reset(m_ref, l_ref, acc_ref)

    def step(k_next, bufs_next, k_cur, bufs_cur, delta=None):
        if k_next is not None:
            s_next, p_next = bufs_next
            s_next[...] = _dot_nt(kvs_ref[0, chunk_rows(k_next), :], q_z)
            expand = jnp.where(e_col == k_next * bpc + e_row, 1.0, 0.0).astype(_MXU_DTYPE)
            p_next[...] = _dot(expand, sel)
        if k_cur is not None:
            s_cur, p_cur = bufs_cur
            addm = jnp.where(p_cur[...] > 0.5, 0.0, NEG)
            s = s_cur[...] + jnp.concatenate([addm] * H, axis=1)
            if delta is not None:
                s = s + bias_ref[delta]
            _online_t(s, m_ref, l_ref, acc_ref, vst_ref[0, k_cur, HEAD_DIM:2 * HEAD_DIM, :])

    _pair_loop(jnp.maximum(qi - 1, 0), step, (sa_ref, pa_ref), (sb_ref, pb_ref))
    _near_chunks(qi, step, (sa_ref, pa_ref), (sb_ref, pb_ref))
    o_slc = _finish_t(m_ref, l_ref, acc_ref)

    _reset(m_ref, l_ref, acc_ref)

    def produce_win(delta, buf):
        buf[...] = _dot_nt(kvw_ref[0, chunk_rows(jnp.maximum(qi - delta, 0)), :], q_z)

    def attend_win(delta, buf):
        kc = qi - delta
        s = buf[...] + wbias_ref[delta] + jnp.where(kc >= 0, 0.0, NEG)
        _online_t(s, m_ref, l_ref, acc_ref,
                  vwt_ref[0, jnp.maximum(kc, 0), HEAD_DIM:2 * HEAD_DIM, :])

    bufs = (sa_ref, sb_ref)
    produce_win(0, bufs[0])
    for delta in range(n_win):
        if delta + 1 < n_win:
            produce_win(delta + 1, bufs[(delta + 1) % 2])
        attend_win(delta, bufs[delta % 2])
    o_win = _finish_t(m_ref, l_ref, acc_ref)

    g = jax.nn.sigmoid(side_ref[0].T[IDX_HEADS:IDX_HEADS + 3 * H])
    outs = []
    for h in range(H):
        sl = slice(h * T, (h + 1) * T)
        outs.append(g[3 * h:3 * h + 1] * o_cmp[:, sl] + g[3 * h + 1:3 * h + 2] * o_slc[:, sl]
                    + g[3 * h + 2:3 * h + 3] * o_win[:, sl])
    o_ref[0] = jnp.concatenate(outs, axis=0).T.astype(o_ref.dtype)


def _nsa(bpack, kvc, vc_t, vs_t, vw_t, side, bias, wbias, overlap_t, T, n_cmp, n_sel):
    Bn, S, _ = bpack.shape
    NC = kvc.shape[1]
    N = B_HEADS * T
    kern = functools.partial(_nsa_kernel, T=T, n_cmp=n_cmp, n_sel=n_sel, n_win=wbias.shape[0])
    vtspec = lambda a: pl.BlockSpec((1,) + a.shape[1:], lambda b, i: (b, 0, 0, 0))
    return pl.pallas_call(
        kern,
        grid=(Bn, S // T),
        in_specs=[pl.BlockSpec((1, T, 256), lambda b, i: (b, i, 0)),
                  pl.BlockSpec((1, NC, 128), lambda b, i: (b, 0, 0)),
                  pl.BlockSpec((1, HEAD_DIM, NC), lambda b, i: (b, 0, 0)),
                  pl.BlockSpec((1, S, 128), lambda b, i: (b, 0, 3)),
                  vtspec(vs_t),
                  pl.BlockSpec((1, S, 128), lambda b, i: (b, 0, 4)),
                  vtspec(vw_t),
                  pl.BlockSpec((1, T, 128), lambda b, i: (b, i, 0)),
                  pl.BlockSpec(bias.shape, lambda b, i: (0, 0, 0)),
                  pl.BlockSpec(wbias.shape, lambda b, i: (0, 0, 0)),
                  pl.BlockSpec(overlap_t.shape, lambda b, i: (0, 0))],
        out_specs=pl.BlockSpec((1, T, 256), lambda b, i: (b, i, 0)),
        out_shape=jax.ShapeDtypeStruct((Bn, S, 256), _MXU_DTYPE),
        scratch_shapes=[pltpu.VMEM((HEAD_DIM, N), F32),
                        pltpu.VMEM((1, N), F32), pltpu.VMEM((1, N), F32),
                        pltpu.VMEM((T, N), F32), pltpu.VMEM((T, N), F32),
                        pltpu.VMEM((T, T), F32), pltpu.VMEM((T, T), F32)],
        compiler_params=_cparams(("parallel", "arbitrary")),
        name="nsa_attention",
    )(bpack, kvc, vc_t, bpack, vs_t, bpack, vw_t, side, bias, wbias, overlap_t)


def _diff_kernel(cq_ref, ck_ref, vt_ref, bias_ref, dl_ref, g_ref, o_ref, acc_ref, m_ref, l_ref,
                 sa_ref, sb_ref, *, T, lam_init):
    qi = pl.program_id(1)
    nk = qi + 1
    H = C_HEADS
    dl = dl_ref[...]
    lam = (jnp.exp(jnp.sum(dl[0:1] * dl[1:2], axis=1, keepdims=True))
           - jnp.exp(jnp.sum(dl[2:3] * dl[3:4], axis=1, keepdims=True)) + lam_init)

    cq = cq_ref[0]
    lane = lax.broadcasted_iota(jnp.int32, (T, 2 * HEAD_DIM), 1)
    q_z = []
    for h in range(H):
        qh = cq[:, h * 2 * HEAD_DIM:(h + 1) * 2 * HEAD_DIM]
        zero = jnp.zeros_like(qh)
        q_z.append(jnp.concatenate([jnp.where(lane < HEAD_DIM, qh, zero),
                                    jnp.where(lane >= HEAD_DIM, qh, zero)], axis=0))

    for h in range(H):
        _reset(m_ref.at[h], l_ref.at[h], acc_ref.at[h])

    def step(k_next, dst, k_cur, src, delta=None):
        if k_next is not None:
            kch = ck_ref[0, pl.ds(pl.multiple_of(k_next * T, T), T), :]
        if k_cur is not None:
            v_t = vt_ref[0, k_cur]
        for h in range(H):
            if k_next is not None:
                dst[h] = _dot_nt(kch[:, h * 2 * HEAD_DIM:(h + 1) * 2 * HEAD_DIM], q_z[h])
            if k_cur is not None:
                s = src[h]
                if delta is not None:
                    b = bias_ref[delta, h]
                    s = jnp.concatenate([s[:, 0:T] + b, s[:, T:2 * T] + b], axis=1)
                _online_t(s, m_ref.at[h], l_ref.at[h], acc_ref.at[h],
                          v_t[h * C_VDIM:(h + 1) * C_VDIM])

    _pair_loop(jnp.maximum(qi - 1, 0), step, sa_ref, sb_ref)
    _near_chunks(qi, step, sa_ref, sb_ref)

    outs = []
    for h in range(H):
        o = _finish_t(m_ref.at[h], l_ref.at[h], acc_ref.at[h])
        oc = o[:, 0:T] - lam * o[:, T:2 * T]
        rms = lax.rsqrt(jnp.mean(jnp.square(oc), axis=0, keepdims=True) + LN_EPS)
        outs.append(oc * rms * g_ref[...] * (1.0 - lam_init))
    o_ref[0] = jnp.concatenate(outs, axis=0).T.astype(o_ref.dtype)


def _diff(cpack, cv_t, bias, dl, g, T, lam_init):
    Bn, S, _ = cpack.shape
    W = C_HEADS * C_VDIM
    kern = functools.partial(_diff_kernel, T=T, lam_init=lam_init)
    return pl.pallas_call(
        kern,
        grid=(Bn, S // T),
        in_specs=[pl.BlockSpec((1, T, W), lambda b, i: (b, i, 0)),
                  pl.BlockSpec((1, S, W), lambda b, i: (b, 0, 1)),
                  pl.BlockSpec((1,) + cv_t.shape[1:], lambda b, i: (b, 0, 0, 0)),
                  pl.BlockSpec(bias.shape, lambda b, i: (0, 0, 0, 0)),
                  pl.BlockSpec(dl.shape, lambda b, i: (0, 0)),
                  pl.BlockSpec(g.shape, lambda b, i: (0, 0))],
        out_specs=pl.BlockSpec((1, T, W), lambda b, i: (b, i, 0)),
        out_shape=jax.ShapeDtypeStruct((Bn, S, W), _MXU_DTYPE),
        scratch_shapes=[pltpu.VMEM((C_HEADS, C_VDIM, 2 * T), F32),
                        pltpu.VMEM((C_HEADS, 1, 2 * T), F32),
                        pltpu.VMEM((C_HEADS, 1, 2 * T), F32),
                        pltpu.VMEM((C_HEADS, T, 2 * T), F32),
                        pltpu.VMEM((C_HEADS, T, 2 * T), F32)],
        compiler_params=_cparams(("parallel", "arbitrary")),
        name="diff_attention",
    )(cpack, cpack, cv_t, bias, dl, g)


def _layer_norm(v, g, b):
    mu = jnp.mean(v, axis=1, keepdims=True)
    d = v - mu
    var = jnp.mean(jnp.square(d), axis=1, keepdims=True)
    return d * lax.rsqrt(var + LN_EPS) * g + b


def _split_hi_lo(v):
    hi = v.astype(_MXU_DTYPE)
    lo = (v - hi.astype(F32)).astype(_MXU_DTYPE)
    return hi, lo


def _outproj_kernel(oa_ref, ob_ref, oc_ref, mg_ref, x_ref, mod1_ref, mod2_ref,
                    wa_ref, wb_ref, wc_ref, wo_ref, ln_ref, rw_ref, rb_ref,
                    x1_ref, h2_ref, tr_ref, tw_ref, cnt_ref, run_ref, *, dn_alpha, n_experts):
    D = x_ref.shape[2]
    tm = x_ref.shape[1]
    first_step = (pl.program_id(0) == 0) & (pl.program_id(1) == 0)

    @pl.when(first_step)
    def _():
        run_ref[...] = jnp.zeros(run_ref.shape, F32)

    ya = _dot(oa_ref[0], wa_ref[...])
    yb = _dot(ob_ref[0], wb_ref[...])
    yc = _dot(oc_ref[0], wc_ref[...])
    mg = mg_ref[0].astype(F32)
    merged = (jax.nn.sigmoid(mg[:, 0:D]) * ya + jax.nn.sigmoid(mg[:, D:2 * D]) * yb
              + jax.nn.sigmoid(mg[:, 2 * D:3 * D]) * yc)
    y = _dot(merged.astype(_MXU_DTYPE), wo_ref[...])
    gate1 = mod1_ref[0, 2:3, :]
    x1 = _layer_norm(dn_alpha * x_ref[0] + (1.0 + gate1) * y, ln_ref[0:1, :], ln_ref[1:2, :])
    x1_ref[0] = x1
    h2 = x1 * (1.0 + mod2_ref[0, 1:2, :]) + mod2_ref[0, 0:1, :]
    h2_ref[0] = h2.astype(h2_ref.dtype)

    h_hi, h_lo = _split_hi_lo(h2)
    w_hi, w_lo = _split_hi_lo(rw_ref[...])
    hw = _dot(h_hi, jnp.concatenate([w_hi, w_lo], axis=1))
    logits = hw[:, 0:LANES] + hw[:, LANES:2 * LANES] + _dot(h_lo, w_hi) + rb_ref[...]
    lane = lax.broadcasted_iota(jnp.int32, logits.shape, 1)
    lanef = lane.astype(F32)
    work = jnp.where(lane < n_experts, logits, -jnp.inf)
    tv = jnp.zeros(logits.shape, F32)
    firsts, onehots = [], []
    v0 = None
    for k in range(TOP_K):
        mx = jnp.max(work, axis=1, keepdims=True)
        first = jnp.min(jnp.where(work == mx, lanef, float(LANES)), axis=1, keepdims=True)
        if k == 0:
            v0 = mx
        tv = jnp.where(lane == k, jnp.exp(mx - v0), tv)
        hit = lanef == first
        work = jnp.where(hit, -jnp.inf, work)
        firsts.append(first)
        onehots.append(jnp.where(hit, 1.0, 0.0))
    tw_ref[0] = tv / jnp.sum(tv, axis=1, keepdims=True)

    chosen = onehots[0] + onehots[1] + onehots[2] + onehots[3]
    r_i = lax.broadcasted_iota(jnp.int32, (tm, tm), 0)
    c_i = lax.broadcasted_iota(jnp.int32, (tm, tm), 1)
    earlier = jnp.where(c_i < r_i, 1.0, 0.0).astype(_MXU_DTYPE)
    base = _dot(earlier, chosen.astype(_MXU_DTYPE)) + run_ref[0:1, :]
    tr = jnp.zeros(logits.shape, F32)
    for k in range(TOP_K):
        rank = jnp.sum(onehots[k] * base, axis=1, keepdims=True)
        tr = jnp.where(lane == k, firsts[k], tr)
        tr = jnp.where(lane == TOP_K + k, rank, tr)
    tr_ref[0] = tr.astype(jnp.int32)
    run_ref[...] = run_ref[...] + jnp.sum(chosen, axis=0, keepdims=True)
    cnt_ref[...] = run_ref[...]


def _outproj(oa, ob, oc, mg, x, mod1, mod2, wa, wb, wc, wo, ln, rw, rb, dn_alpha, n_experts):
    Bn, S, D = x.shape
    tm = TM_PROJ
    row = lambda w: pl.BlockSpec((1, tm, w), lambda b, i: (b, i, 0))
    full = lambda a: pl.BlockSpec(a.shape, lambda b, i: (0,) * a.ndim)
    modspec = pl.BlockSpec((1, 3, D), lambda b, i: (b, 0, 0))
    kern = functools.partial(_outproj_kernel, dn_alpha=dn_alpha, n_experts=n_experts)
    return pl.pallas_call(
        kern,
        grid=(Bn, S // tm),
        in_specs=[row(oa.shape[2]), row(ob.shape[2]), row(oc.shape[2]), row(mg.shape[2]), row(D),
                  modspec, modspec, full(wa), full(wb), full(wc), full(wo), full(ln), full(rw),
                  full(rb)],
        out_specs=[row(D), row(D), row(LANES), row(LANES),
                   pl.BlockSpec((SUBLANES, LANES), lambda b, i: (0, 0))],
        out_shape=[jax.ShapeDtypeStruct((Bn, S, D), F32),
                   jax.ShapeDtypeStruct((Bn, S, D), _MXU_DTYPE),
                   jax.ShapeDtypeStruct((Bn, S, LANES), jnp.int32),
                   jax.ShapeDtypeStruct((Bn, S, LANES), F32),
                   jax.ShapeDtypeStruct((SUBLANES, LANES), F32)],
        scratch_shapes=[pltpu.VMEM((SUBLANES, LANES), F32)],
        compiler_params=_cparams(("arbitrary", "arbitrary")),
        name="merge_outproj_ln_router",
    )(oa, ob, oc, mg, x, mod1, mod2, wa, wb, wc, wo, ln, rw, rb)


def _deinterleave_kernel(w_ref, p_ref, og_ref, ou_ref):
    n = w_ref.shape[1]
    for j in range(n // (2 * LANES)):
        blk = w_ref[:, j * 2 * LANES:(j + 1) * 2 * LANES].astype(_MXU_DTYPE)
        r = _dot(blk, p_ref[...])
        og_ref[:, j * LANES:(j + 1) * LANES] = r[:, 0:LANES].astype(og_ref.dtype)
        ou_ref[:, j * LANES:(j + 1) * LANES] = r[:, LANES:2 * LANES].astype(ou_ref.dtype)


def _deinterleave(w_all, part, n_parts):
    N2 = w_all.shape[1]
    R = w_all.shape[0] // n_parts
    tr = 2048
    first = part * (R // tr)
    perm = np.zeros((2 * LANES, 2 * LANES), np.float32)
    perm[2 * np.arange(LANES), np.arange(LANES)] = 1.0
    perm[2 * np.arange(LANES) + 1, LANES + np.arange(LANES)] = 1.0
    return pl.pallas_call(
        _deinterleave_kernel,
        grid=(R // tr,),
        in_specs=[pl.BlockSpec((tr, N2), lambda i: (first + i, 0)),
                  pl.BlockSpec(perm.shape, lambda i: (0, 0))],
        out_specs=[pl.BlockSpec((tr, N2 // 2), lambda i: (i, 0))] * 2,
        out_shape=[jax.ShapeDtypeStruct((R, N2 // 2), _MXU_DTYPE)] * 2,
        compiler_params=_cparams(("parallel",)),
        name="expert_weight_deinterleave",
    )(w_all, jnp.asarray(perm).astype(_MXU_DTYPE))


def _moe_kernel(te_ref, tv_ref, x_ref, wg_ref, wu_ref, wd_ref, bg_ref, bu_ref, bd_ref, o_ref):
    i = pl.program_id(0)

    @pl.when(tv_ref[i] > 0)
    def _():
        x = x_ref[...]
        dff = wg_ref.shape[2]
        ck = 512
        y = jnp.zeros(o_ref.shape, F32)
        for j in range(0, dff, ck):
            gate = jnp.minimum(_dot(x, wg_ref[0, :, j:j + ck]) + bg_ref[0, :, j:j + ck], SWIGLU_LIMIT)
            up = jnp.clip(_dot(x, wu_ref[0, :, j:j + ck]) + bu_ref[0, :, j:j + ck],
                          -SWIGLU_LIMIT, SWIGLU_LIMIT)
            act = (up + 1.0) * (gate * jax.nn.sigmoid(SWIGLU_ALPHA * gate))
            y = y + _dot(act.astype(_MXU_DTYPE), wd_ref[0, j:j + ck, :])
        o_ref[...] = (y + bd_ref[0]).astype(o_ref.dtype)


def _moe(xs, tile_expert, tile_valid, wg, wu, wd_all, wd_first, bg, bu, bd):
    R, D = xs.shape
    tm = TM_MOE
    nt = R // tm
    wspec = lambda a: pl.BlockSpec((1,) + a.shape[1:], lambda i, te, tv: (te[i], 0, 0))
    wdspec = pl.BlockSpec((1,) + wd_all.shape[1:], lambda i, te, tv: (wd_first + te[i], 0, 0))
    grid_spec = pltpu.PrefetchScalarGridSpec(
        num_scalar_prefetch=2,
        grid=(nt,),
        in_specs=[pl.BlockSpec((tm, D), lambda i, te, tv: (i, 0)),
                  wspec(wg), wspec(wu), wdspec, wspec(bg), wspec(bu), wspec(bd)],
        out_specs=pl.BlockSpec((tm, D), lambda i, te, tv: (i, 0)),
    )
    return pl.pallas_call(
        _moe_kernel,
        grid_spec=grid_spec,
        out_shape=jax.ShapeDtypeStruct((R, D), _MXU_DTYPE),
        compiler_params=_cparams(("arbitrary",)),
        name="moe_grouped_mlp",
    )(tile_expert, tile_valid, xs, wg, wu, wd_all, bg, bu, bd)


def _combine_kernel(yg_ref, tw_ref, x_ref, mod_ref, ln_ref, o_ref, *, dn_alpha):
    tw = tw_ref[0]
    y = tw[:, 0:1] * yg_ref[0, 0].astype(F32)
    for k in range(1, TOP_K):
        y = y + tw[:, k:k + 1] * yg_ref[k, 0].astype(F32)
    gate = mod_ref[0, 2:3, :]
    o_ref[0] = _layer_norm(dn_alpha * x_ref[0] + (1.0 + gate) * y, ln_ref[0:1, :], ln_ref[1:2, :])


def _combine(yg, tw, x, mod, ln, dn_alpha):
    Bn, S, D = x.shape
    tm = TM_PROJ
    row = lambda w: pl.BlockSpec((1, tm, w), lambda b, i: (b, i, 0))
    return pl.pallas_call(
        functools.partial(_combine_kernel, dn_alpha=dn_alpha),
        grid=(Bn, S // tm),
        in_specs=[pl.BlockSpec((TOP_K, 1, tm, D), lambda b, i: (0, b, i, 0)), row(LANES), row(D),
                  pl.BlockSpec((1, 3, D), lambda b, i: (b, 0, 0)),
                  pl.BlockSpec(ln.shape, lambda b, i: (0, 0))],
        out_specs=row(D),
        out_shape=jax.ShapeDtypeStruct((Bn, S, D), F32),
        compiler_params=_cparams(("parallel", "parallel")),
        name="moe_combine_ln",
    )(yg, tw, x, mod, ln)


def _bucket_np(dist):
    exact = N_BUCKETS // 2
    n = np.maximum(dist, 0)
    nf = np.maximum(n, 1).astype(np.float32)
    large = exact + (np.log(nf / exact) / math.log(MAX_DISTANCE / exact)
                     * (N_BUCKETS - exact)).astype(np.int32)
    large = np.minimum(large, N_BUCKETS - 1)
    return np.where(n < exact, n, large)


def _bias_tiles_t(tab, T, n_delta, window=None, shift_far=True):
    H = tab.shape[1]
    L = 2 * T
    k = np.arange(L)
    diff = np.where(k < T, k, k - L)
    d = diff[None, :] + (np.arange(n_delta) * T)[:, None]
    ok = d >= 0 if window is None else (d >= 0) & (d < window)
    onehot = np.eye(N_BUCKETS, dtype=np.float32)[_bucket_np(d)]
    prof = jnp.einsum("dlb,bh->dhl", jnp.asarray(onehot), tab, precision=lax.Precision.HIGHEST)
    if shift_far:
        prof = prof - tab[N_BUCKETS - 1][None, :, None]
    prof = jnp.where(jnp.asarray(ok)[:, None, :], prof * LOG2E, NEG)
    rolled = jnp.tile(prof, (1, 1, T))[:, :, :T * (L - 1)].reshape(n_delta, H, T, L - 1)
    return rolled[:, :, :, :T]


def _stack_tiles(tiles):
    n, H, T, _ = tiles.shape
    return jnp.transpose(tiles, (0, 2, 1, 3)).reshape(n, T, H * T)


def _slc_overlap_np(n_cmp, n_slc, rows):
    start = np.arange(n_cmp) * CMP_STRIDE
    end = start + CMP_BLOCK
    bs = np.arange(n_slc) * SLC_BLOCK
    ov = (start[:, None] < bs[None, :] + SLC_BLOCK) & (end[:, None] > bs[None, :])
    out = np.zeros((rows, n_slc), np.float32)
    out[:n_cmp] = ov
    return out


def _routing(top_i, rank, counts, tm):
    N = top_i.shape[0]
    E = counts.shape[0]
    A = N * TOP_K
    nt = A // tm + E
    padded = ((counts + tm - 1) // tm) * tm
    starts = jnp.cumsum(counts) - counts
    pstarts = jnp.cumsum(padded) - padded
    pend = pstarts + padded
    tile_start = jnp.arange(nt, dtype=jnp.int32) * tm
    tile_valid = (tile_start < pend[-1]).astype(jnp.int32)
    te = jnp.minimum(jnp.sum((tile_start[:, None] >= pend[None, :]).astype(jnp.int32), axis=1), E - 1)
    last_valid = jnp.max(jnp.where(tile_valid > 0, te, 0))
    tile_expert = jnp.where(tile_valid > 0, te, last_valid).astype(jnp.int32)
    onehot_t = tile_expert[:, None] == jnp.arange(E)[None, :]
    pick = lambda tab: jnp.sum(jnp.where(onehot_t, tab[None, :], 0), axis=1)
    order = jnp.argsort(top_i.reshape(A), stable=True).astype(jnp.int32)
    in_group = (tile_start - pick(pstarts))[:, None] + jnp.arange(tm, dtype=jnp.int32)[None, :]
    ok = (in_group < pick(counts)[:, None]) & (tile_valid[:, None] > 0)
    slot = jnp.clip(pick(starts)[:, None] + in_group, 0, A - 1)
    filler = (jnp.arange(nt * tm, dtype=jnp.int32) % N).reshape(nt, tm)
    row_tok = jnp.where(ok, jnp.take(order, slot.reshape(-1), mode="clip").reshape(nt, tm) // TOP_K,
                        filler)
    onehot_a = top_i[:, :, None] == jnp.arange(E)[None, None, :]
    pos = rank + jnp.sum(jnp.where(onehot_a, pstarts[None, None, :], 0), axis=2)
    return tile_expert, tile_valid, row_tok.reshape(nt * tm), pos.astype(jnp.int32)


def kernel(x, c, rel_bias, mod_attn_w, mod_attn_b, w_in, cmp_pos, cmp_w1, cmp_w2, diff_lambda,
           diff_norm_g, w_branch_a, w_branch_b, w_branch_c, w_out, ln1_g, ln1_b, mod_ffn_w,
           mod_ffn_b, router_w, router_b, exp_w_gu, exp_b_gu, exp_w_down, exp_b_down, ln2_g, ln2_b):
    Bn, S, D = x.shape
    L = w_in.shape[0]
    E = exp_w_gu.shape[1]
    T = T_ATT
    cdt = _MXU_DTYPE
    dn_alpha = (2 * L) ** 0.25
    assert S % T == 0 and S % TM_PROJ == 0 and (Bn * S * TOP_K) % TM_MOE == 0
    assert (S // SLC_BLOCK) % SUBLANES == 0 and E <= LANES and T >= MAX_DISTANCE

    mod_w = jnp.stack([mod_attn_w, mod_ffn_w], axis=1).reshape(2 * L, D, 3 * D)
    mod_b = jnp.stack([mod_attn_b, mod_ffn_b], axis=1).reshape(2 * L, 3 * D)
    mods = _adaln(c, mod_w, mod_b).reshape(2 * L, Bn, 3, D)

    n_win = WINDOW // T + 1
    tab_b = rel_bias[:, A_HEADS:A_HEADS + B_HEADS]
    bias_a = _stack_tiles(_bias_tiles_t(rel_bias[:, :A_HEADS], T, 2))
    bias_b = _stack_tiles(_bias_tiles_t(tab_b, T, 2))
    bias_w = _stack_tiles(_bias_tiles_t(tab_b, T, n_win, WINDOW, shift_far=False))
    bias_c = _bias_tiles_t(rel_bias[:, A_HEADS + B_HEADS:], T, 2)
    jj = np.arange(T)
    cmask = jnp.asarray(np.where(jj[None, :] >= jj[:, None], 0.0, -np.inf).astype(np.float32))

    n_cmp = (S - CMP_BLOCK) // CMP_STRIDE + 1
    n_slc = S // SLC_BLOCK
    n_sel = min(SLC_TOPN, n_slc)
    NC = S // CMP_STRIDE
    overlap_t = jnp.asarray(_slc_overlap_np(n_cmp, n_slc, NC).T).astype(cdt)

    rows16 = CMP_STRIDE * 2 * HEAD_DIM
    w_gu_all = exp_w_gu.reshape(L * E * D, -1)
    w_down_all = exp_w_down.astype(cdt).reshape(L * E, exp_w_down.shape[2], D)

    for l in range(L):
        lam_init = 0.8 - 0.6 * math.exp(-0.3 * l)
        wl = w_in[l]
        wa = jnp.pad(jnp.concatenate([wl[:, 0:256] * Q_FOLD, wl[:, 256:544]], axis=1),
                     ((0, 0), (0, 96))).astype(cdt)
        wb = jnp.concatenate([wl[:, 548:804] * Q_FOLD, wl[:, 804:1188]], axis=1).astype(cdt)
        wc = jnp.concatenate([wl[:, 1200:1712] * Q_FOLD, wl[:, 1712:2736]], axis=1).astype(cdt)
        ws = jnp.pad(jnp.concatenate([wl[:, 544:548], wl[:, 1188:1200]], axis=1),
                     ((0, 0), (0, LANES - 16))).astype(cdt)
        wg = wl[:, 2736:].astype(cdt)

        apack, bpack, cpack, side, mg, akv_t, kvs_t, kvw_t, cv_t = _inproj(
            x, mods[2 * l], wa, wb, wc, ws, wg, T)

        r2 = bpack[:, :, 256:384].reshape(Bn, NC, rows16)
        r2s = jnp.concatenate([r2[:, 1:], jnp.zeros_like(r2[:, :1])], axis=1)
        half = CMP_BLOCK // 2
        zpad = jnp.zeros((half, HEAD_DIM), F32)
        pa, pb, wca, wcb = [], [], [], []
        for j in range(2):
            pos = cmp_pos[l, j]
            w1 = cmp_w1[l, j].reshape(CMP_BLOCK, HEAD_DIM, CMP_HIDDEN)
            zw = jnp.zeros((half, HEAD_DIM, CMP_HIDDEN), F32)
            kv = (lambda a, z: jnp.concatenate([a, z], axis=1)) if j == 0 else \
                 (lambda a, z: jnp.concatenate([z, a], axis=1))
            pa.append(kv(pos[:half], zpad).reshape(1, rows16))
            pb.append(kv(pos[half:], zpad).reshape(1, rows16))
            wca.append(kv(w1[:half], zw).reshape(rows16, CMP_HIDDEN))
            wcb.append(kv(w1[half:], zw).reshape(rows16, CMP_HIDDEN))
        kvc = _compress(r2, r2s, jnp.stack(pa), jnp.stack(pb), jnp.stack(wca).astype(cdt),
                        jnp.stack(wcb).astype(cdt), cmp_w2[l].astype(cdt))

        oa = _dsa(apack, akv_t, side, bias_a, cmask, T)
        ob = _nsa(bpack, kvc, jnp.transpose(kvc[:, :, HEAD_DIM:], (0, 2, 1)), kvs_t, kvw_t,
                  side, bias_b, bias_w, overlap_t, T, n_cmp, n_sel)
        oc = _diff(cpack, cv_t, bias_c, diff_lambda[l], diff_norm_g[l].reshape(C_VDIM, 1), T,
                   lam_init)

        rw = jnp.pad(router_w[l], ((0, 0), (0, LANES - E)))
        rb = jnp.pad(router_b[l], (0, LANES - E)).reshape(1, LANES)
        x1, h2, tr, tw, cnt = _outproj(
            oa, ob, oc, mg, x, mods[2 * l], mods[2 * l + 1],
            w_branch_a[l].astype(cdt), w_branch_b[l].astype(cdt), w_branch_c[l].astype(cdt),
            w_out[l].astype(cdt), jnp.stack([ln1_g[l], ln1_b[l]]), rw, rb, dn_alpha, E)

        N = Bn * S
        tr = tr.reshape(N, LANES)
        tile_expert, tile_valid, row_tok, pos = _routing(
            tr[:, 0:TOP_K], tr[:, TOP_K:2 * TOP_K], cnt[0, :E].astype(jnp.int32), TM_MOE)
        xs = jnp.take(h2.reshape(N, D), row_tok, axis=0, mode="clip")
        wgate, wup = _deinterleave(w_gu_all, l, L)
        bgu = exp_b_gu[l].reshape(E, 1, -1, 2)
        ys = _moe(xs, tile_expert, tile_valid,
                  wgate.reshape(E, D, -1), wup.reshape(E, D, -1), w_down_all, l * E,
                  bgu[..., 0], bgu[..., 1], exp_b_down[l].reshape(E, 1, D))
        yg = jnp.take(ys, pos.T.reshape(-1), axis=0, mode="clip").reshape(TOP_K, Bn, S, D)
        x = _combine(yg, tw, x1, mods[2 * l + 1], jnp.stack([ln2_g[l], ln2_b[l]]), dn_alpha)
    return x
```
